```python
import jax, jax.numpy as jnp
from jax import lax
import numpy as np

D_MODEL = 1024
BATCH = 2
SEQ = 16384
DEPTH = 2

GRID_W = 64
CTX_LEN = 256
N_EVEN = (DEPTH + 1) // 2
N_ODD = DEPTH // 2
HEAD_DIM = 64
ATTN_HEADS = 8
KV_HEADS = 2
GQA_GROUP = ATTN_HEADS // KV_HEADS
WINDOW = 128
ATTN_BLOCK = 128
ROPE_BASE = 10000.0
GMLP_GROUPS = 8
GMLP_DIM = 64
CHUNK = 128
Q_W = ATTN_HEADS * HEAD_DIM
KV_W = KV_HEADS * HEAD_DIM
A_W = GMLP_GROUPS * GMLP_DIM
IN_W = Q_W + 2 * KV_W + 2 * A_W
MIX_W = Q_W + A_W
POOL_WINDOWS = (2, 4, 8, 16)
POOL_GROUPS = 4
POOL_DIM = D_MODEL // POOL_GROUPS
PEER_HEADS = 8
PEER_NKEYS = 128
PEER_EXPERTS = PEER_NKEYS * PEER_NKEYS
PEER_TOPK = 16
PEER_DKEY = 256
PEER_DHALF = PEER_DKEY // 2
PEER_BLOCK = 128
N_MOD = 6
EPS = 1e-6

kernel_name = 'hybrid_swa_gmlp_pool_peer_dit'


def rms_norm(x, g):
    xf = x.astype(jnp.float32)
    y = xf * lax.rsqrt(jnp.mean(jnp.square(xf), axis=-1, keepdims=True) + EPS)
    return (y * g).astype(x.dtype)


def axial_rope(t, row, col):
    half = HEAD_DIM // 2
    quarter = half // 2
    inv = jnp.power(ROPE_BASE, -jnp.arange(quarter, dtype=jnp.float32) / quarter)

    def rot(xa, pos):
        ang = pos[:, None] * inv[None, :]
        cos = jnp.cos(ang)[None, :, None, :]
        sin = jnp.sin(ang)[None, :, None, :]
        x1, x2 = xa[..., :quarter], xa[..., quarter:]
        return jnp.concatenate([x1 * cos - x2 * sin, x1 * sin + x2 * cos], axis=-1)

    tf = t.astype(jnp.float32)
    return jnp.concatenate([rot(tf[..., :half], row), rot(tf[..., half:], col)], axis=-1).astype(t.dtype)


def sink_softmax(logits, sink):
    sk = jnp.broadcast_to(sink.astype(jnp.float32).reshape(1, KV_HEADS, GQA_GROUP, 1, 1), logits.shape[:-1] + (1,))
    return jax.nn.softmax(jnp.concatenate([logits, sk], axis=-1), axis=-1)[..., :-1]


def window_attention(q, k, v, k_ctx, v_ctx, sink):
    B_, L_ = q.shape[0], q.shape[1]
    nb = L_ // ATTN_BLOCK
    span = ATTN_BLOCK + 2 * WINDOW
    scale = HEAD_DIM ** -0.5
    pad = ((0, 0), (WINDOW, WINDOW), (0, 0), (0, 0))
    kp = jnp.pad(k, pad)
    vp = jnp.pad(v, pad)
    qb = q.reshape(B_, nb, ATTN_BLOCK, KV_HEADS, GQA_GROUP, HEAD_DIM).swapaxes(0, 1)

    def one_block(args):
        n, qn = args
        start = n * ATTN_BLOCK
        kn = lax.dynamic_slice_in_dim(kp, start, span, axis=1)
        vn = lax.dynamic_slice_in_dim(vp, start, span, axis=1)
        qpos = start + jnp.arange(ATTN_BLOCK)
        kpos = start - WINDOW + jnp.arange(span)
        valid = (jnp.abs(qpos[:, None] - kpos[None, :]) <= WINDOW) & (kpos >= 0)[None, :] & (kpos < L_)[None, :]
        s_loc = jnp.einsum('bqhgd,bkhd->bhgqk', qn, kn).astype(jnp.float32) * scale
        s_loc = jnp.where(valid, s_loc, -jnp.inf)
        s_ctx = jnp.einsum('bqhgd,bkhd->bhgqk', qn, k_ctx).astype(jnp.float32) * scale
        p = sink_softmax(jnp.concatenate([s_loc, s_ctx], axis=-1), sink)
        o = (jnp.einsum('bhgqk,bkhd->bqhgd', p[..., :span], vn.astype(jnp.float32))
             + jnp.einsum('bhgqk,bkhd->bqhgd', p[..., span:], v_ctx.astype(jnp.float32)))
        return o.astype(q.dtype)

    out = lax.map(one_block, (jnp.arange(nb), qb))
    return out.swapaxes(0, 1).reshape(B_, L_, Q_W)


def context_attention(qc, kc, vc, sink):
    B_, C_ = qc.shape[0], qc.shape[1]
    qg = qc.reshape(B_, C_, KV_HEADS, GQA_GROUP, HEAD_DIM)
    s = jnp.einsum('bqhgd,bkhd->bhgqk', qg, kc).astype(jnp.float32) * (HEAD_DIM ** -0.5)
    p = sink_softmax(s, sink)
    o = jnp.einsum('bhgqk,bkhd->bqhgd', p, vc.astype(jnp.float32))
    return o.astype(qc.dtype).reshape(B_, C_, Q_W)


def chunk_gmlp(ua, va, w_s, b_s):
    B_, L_ = ua.shape[0], ua.shape[1]
    u = jax.nn.gelu(ua).reshape(B_, L_, GMLP_GROUPS, GMLP_DIM)
    vf = jax.nn.gelu(va).reshape(B_, L_, GMLP_GROUPS, GMLP_DIM).astype(jnp.float32)
    mu = jnp.mean(vf, axis=-1, keepdims=True)
    var = jnp.mean(jnp.square(vf - mu), axis=-1, keepdims=True)
    v = ((vf - mu) * lax.rsqrt(var + EPS)).astype(ua.dtype)
    vc = v.reshape(B_, L_ // CHUNK, CHUNK, GMLP_GROUPS, GMLP_DIM)
    mixed = jnp.einsum('gpq,bcqgd->bcpgd', w_s, vc) + b_s.T[:, :, None]
    return (u * mixed.reshape(B_, L_, GMLP_GROUPS, GMLP_DIM)).reshape(B_, L_, A_W)


def split_proj(p):
    return jnp.split(p, [Q_W, Q_W + KV_W, Q_W + 2 * KV_W, Q_W + 2 * KV_W + A_W], axis=-1)


def attn_gmlp_mixer(h, hc, w_in, w_out, sink, w_s, b_s, row, col, need_ctx_out):
    B_, L_ = h.shape[0], h.shape[1]
    C_ = hc.shape[1]
    q, k, v, ua, va = split_proj(h @ w_in)
    q = axial_rope(q.reshape(B_, L_, ATTN_HEADS, HEAD_DIM), row, col)
    k = axial_rope(k.reshape(B_, L_, KV_HEADS, HEAD_DIM), row, col)
    v = v.reshape(B_, L_, KV_HEADS, HEAD_DIM)
    if need_ctx_out:
        qc, kc, vc, uac, vac = split_proj(hc @ w_in)
    else:
        kc, vc = jnp.split(hc @ w_in[:, Q_W:Q_W + 2 * KV_W], 2, axis=-1)
    kc = kc.reshape(B_, C_, KV_HEADS, HEAD_DIM)
    vc = vc.reshape(B_, C_, KV_HEADS, HEAD_DIM)
    attn = window_attention(q, k, v, kc, vc, sink)
    mix = chunk_gmlp(ua, va, w_s, b_s)
    y = jnp.concatenate([attn, mix], axis=-1) @ w_out
    yc = None
    if need_ctx_out:
        attn_c = context_attention(qc.reshape(B_, C_, ATTN_HEADS, HEAD_DIM), kc, vc, sink)
        mix_c = chunk_gmlp(uac, vac, w_s, b_s)
        yc = jnp.concatenate([attn_c, mix_c], axis=-1) @ w_out
    return y, yc


def multiscale_pool(h, w_pool, scale):
    B_, L_ = h.shape[0], h.shape[1]
    hf = h.reshape(B_, L_, POOL_GROUPS, POOL_DIM).astype(jnp.float32)
    cs = jnp.pad(lax.cumsum(hf, axis=1), ((0, 0), (1, 0), (0, 0), (0, 0)))
    t = jnp.arange(L_)
    diffs = []
    for g, w in enumerate(POOL_WINDOWS):
        lo = jnp.clip(t - w // 2, 0, L_)
        hi = jnp.clip(t + w // 2, 0, L_)
        mean = (cs[:, hi, g] - cs[:, lo, g]) / (hi - lo).astype(jnp.float32)[None, :, None]
        diffs.append(mean - hf[:, :, g])
    pooled = jnp.stack(diffs, axis=2).astype(h.dtype)
    y = jnp.einsum('blgc,gce->blge', pooled, w_pool).reshape(B_, L_, D_MODEL)
    return y * scale


def peer_ffn(h, w_q, sub_keys, down, up):
    shape = h.shape
    blocks = h.reshape(-1, PEER_BLOCK, D_MODEL)

    def one_block(xb):
        n = xb.shape[0]
        q = (xb @ w_q).reshape(n, PEER_HEADS, 2, PEER_DHALF)
        s = jnp.einsum('thpk,pnk->thpn', q, sub_keys).astype(jnp.float32)
        s_top, i_top = lax.top_k(s, PEER_TOPK)
        cand_s = s_top[:, :, 0, :, None] + s_top[:, :, 1, None, :]
        cand_i = i_top[:, :, 0, :, None] * PEER_NKEYS + i_top[:, :, 1, None, :]
        best_s, best_pos = lax.top_k(cand_s.reshape(n, PEER_HEADS, -1), PEER_TOPK)
        idx = jnp.take_along_axis(cand_i.reshape(n, PEER_HEADS, -1), best_pos, axis=-1)
        gate = jax.nn.softmax(best_s, axis=-1)
        u = jnp.take(down, idx, axis=0)
        act = jax.nn.gelu(jnp.einsum('thkd,td->thk', u, xb))
        v = jnp.take(up, idx, axis=0)
        return jnp.einsum('thk,thkd->td', (gate * act).astype(v.dtype), v)

    return lax.map(one_block, blocks).reshape(shape)


def setup_inputs(seed: int = 0) -> dict:
    key = jax.random.key(seed)
    ks = jax.random.split(key, 20)
    D = D_MODEL

    def nrm(k, shape, s):
        return jax.random.normal(k, shape, jnp.float32) * s

    return {
        'x': nrm(ks[0], (BATCH, SEQ, D), 1.0),
        'c': nrm(ks[1], (BATCH, D), 1.0),
        'ctx': nrm(ks[2], (BATCH, CTX_LEN, D), 1.0),
        'c_ctx': nrm(ks[3], (D,), 1.0),
        'mod_w': nrm(ks[4], (DEPTH, D, N_MOD * D), 0.5 * D ** -0.5),
        'mod_b': nrm(ks[5], (DEPTH, N_MOD * D), 0.02),
        'norm1_g': 1.0 + nrm(ks[6], (DEPTH, D), 0.02),
        'norm2_g': 1.0 + nrm(ks[7], (DEPTH, D), 0.02),
        'attn_in_w': nrm(ks[8], (N_EVEN, D, IN_W), D ** -0.5),
        'attn_out_w': nrm(ks[9], (N_EVEN, MIX_W, D), MIX_W ** -0.5),
        'attn_sink': nrm(ks[10], (N_EVEN, ATTN_HEADS), 0.5),
        'gmlp_w_s': nrm(ks[11], (N_EVEN, GMLP_GROUPS, CHUNK, CHUNK), CHUNK ** -0.5),
        'gmlp_b_s': 1.0 + nrm(ks[12], (N_EVEN, GMLP_GROUPS, CHUNK), 0.02),
        'pool_w': nrm(ks[13], (N_ODD, POOL_GROUPS, POOL_DIM, POOL_DIM), POOL_DIM ** -0.5),
        'pool_scale': 1.0 + nrm(ks[14], (N_ODD, D), 0.02),
        'peer_w_q': nrm(ks[15], (DEPTH, D, PEER_HEADS * PEER_DKEY), D ** -0.5),
        'peer_sub_keys': nrm(ks[16], (DEPTH, 2, PEER_NKEYS, PEER_DHALF), PEER_DHALF ** -0.5),
        'peer_down': nrm(ks[17], (DEPTH, PEER_EXPERTS, D), D ** -0.5),
        'peer_up': nrm(ks[18], (DEPTH, PEER_EXPERTS, D), 0.5),
        'final_g': 1.0 + nrm(ks[19], (D,), 0.02),
    }


def reference(x, c, ctx, c_ctx, mod_w, mod_b, norm1_g, norm2_g, attn_in_w, attn_out_w, attn_sink,
              gmlp_w_s, gmlp_b_s, pool_w, pool_scale, peer_w_q, peer_sub_keys, peer_down, peer_up, final_g):
    L_ = x.shape[1]
    ROWS = L_ // GRID_W
    t = jnp.arange(ROWS * GRID_W)
    row = (t // GRID_W).astype(jnp.float32)
    col = (t % GRID_W).astype(jnp.float32)
    last_reader = ((DEPTH - 1) // 2) * 2
    xc = ctx
    for l in range(DEPTH):
        has_ctx = l <= last_reader
        keep_ctx = l < last_reader
        m = jax.nn.silu(c) @ mod_w[l] + mod_b[l]
        sh1, sc1, g1, sh2, sc2, g2 = jnp.split(m[:, None, :], N_MOD, axis=-1)
        h = rms_norm(x, norm1_g[l]) * (1.0 + sc1) + sh1
        hc = None
        if has_ctx:
            mc = jax.nn.silu(c_ctx) @ mod_w[l] + mod_b[l]
            csh1, csc1, cg1, csh2, csc2, cg2 = jnp.split(mc, N_MOD, axis=-1)
            hc = rms_norm(xc, norm1_g[l]) * (1.0 + csc1) + csh1
        if l % 2 == 0:
            e = l // 2
            y, yc = attn_gmlp_mixer(h, hc, attn_in_w[e], attn_out_w[e], attn_sink[e],
                                    gmlp_w_s[e], gmlp_b_s[e], row, col, keep_ctx)
        else:
            o = l // 2
            y = multiscale_pool(h, pool_w[o], pool_scale[o])
            yc = multiscale_pool(hc, pool_w[o], pool_scale[o]) if keep_ctx else None
        x = x + g1 * y
        h2 = rms_norm(x, norm2_g[l]) * (1.0 + sc2) + sh2
        x = x + g2 * peer_ffn(h2, peer_w_q[l], peer_sub_keys[l], peer_down[l], peer_up[l])
        if keep_ctx:
            xc = xc + cg1 * yc
            hc2 = rms_norm(xc, norm2_g[l]) * (1.0 + csc2) + csh2
            xc = xc + cg2 * peer_ffn(hc2, peer_w_q[l], peer_sub_keys[l], peer_down[l], peer_up[l])
    return rms_norm(x, final_g)
```

```python
import functools

import jax
import jax.numpy as jnp
from jax import lax
from jax.experimental import pallas as pl
from jax.experimental.pallas import tpu as pltpu

D_MODEL = 1024
GRID_W = 64
HEAD_DIM = 64
ATTN_HEADS = 8
KV_HEADS = 2
GQA_GROUP = ATTN_HEADS // KV_HEADS
WINDOW = 128
ATTN_BLOCK = 128
ROPE_BASE = 10000.0
GMLP_GROUPS = 8
GMLP_DIM = 64
CHUNK = 128
Q_W = ATTN_HEADS * HEAD_DIM
KV_W = KV_HEADS * HEAD_DIM
A_W = GMLP_GROUPS * GMLP_DIM
POOL_WINDOWS = (2, 4, 8, 16)
POOL_GROUPS = 4
POOL_DIM = D_MODEL // POOL_GROUPS
POOL_HALO = 8
PEER_HEADS = 8
PEER_NKEYS = 128
PEER_TOPK = 16
PEER_DKEY = 256
PEER_DHALF = PEER_DKEY // 2
PEER_SEL = PEER_HEADS * PEER_TOPK
N_MOD = 6
EPS = 1e-6

LANES = 128
SUBLANES = 8
QPAD_W = ATTN_HEADS * LANES
MXU_DTYPE = jnp.bfloat16
NEG = -1e30
F32 = jnp.float32

TM_DENSE = 256
TB_GATHER = 16
VMEM_LIMIT = 48 * 1024 * 1024


def _rms(x):
    return x * lax.rsqrt(jnp.mean(x * x, axis=-1, keepdims=True) + EPS)


def _dot(a, b):
    return jnp.dot(a, b, preferred_element_type=F32)


def _dot_nt(a, b):
    return lax.dot_general(a, b, (((1,), (1,)), ((), ())), preferred_element_type=F32)


def _mod_kernel(c_ref, w_ref, b_ref, o_ref):
    c = c_ref[...]
    s = c * jax.nn.sigmoid(c)
    o_ref[0] = jnp.dot(s, w_ref[0], preferred_element_type=F32, precision=lax.Precision.HIGHEST) + b_ref[0]


def _modulation(cs, mod_w, mod_b):
    depth, d, n6 = mod_w.shape
    r = cs.shape[0]
    tn = 512
    out = pl.pallas_call(
        _mod_kernel,
        grid=(depth, n6 // tn),
        in_specs=[pl.BlockSpec((r, d), lambda l, j: (0, 0)),
                  pl.BlockSpec((1, d, tn), lambda l, j: (l, 0, j)),
                  pl.BlockSpec((1, 1, tn), lambda l, j: (l, 0, j))],
        out_specs=pl.BlockSpec((1, r, tn), lambda l, j: (l, 0, j)),
        out_shape=jax.ShapeDtypeStruct((depth, r, n6), F32),
        name="adaln_modulation",
    )(cs, mod_w, mod_b.reshape(depth, 1, n6))
    return out.reshape(depth, r, N_MOD, d)


def _rope(t, cos, sin, first_half):
    partner = jnp.where(first_half, pltpu.roll(t, LANES - 16, 1), pltpu.roll(t, 16, 1))
    return t * cos + partner * sin


def _inproj_kernel(x_ref, mod_ref, g_ref, w_ref, cos_ref, sin_ref, m64_ref,
                   q_ref, k_ref, v_ref, u_ref, vn_ref):
    x = x_ref[...]
    sh = mod_ref[0, 0:1, :]
    sc = mod_ref[0, 1:2, :]
    h = _rms(x) * g_ref[...] * (1.0 + sc) + sh
    p = _dot(h.astype(MXU_DTYPE), w_ref[...])
    cos = cos_ref[...]
    sin = sin_ref[...]
    lane = lax.broadcasted_iota(jnp.int32, cos.shape, 1)
    first_half = (lane & 31) < 16
    scale = HEAD_DIM ** -0.5
    for s in range(ATTN_HEADS):
        t = _rope(p[:, s * LANES:(s + 1) * LANES], cos, sin, first_half)
        q_ref[:, s * LANES:(s + 1) * LANES] = (t * scale).astype(q_ref.dtype)
    o = QPAD_W
    k_ref[...] = _rope(p[:, o:o + KV_W], cos, sin, first_half).astype(k_ref.dtype)
    v_ref[...] = p[:, o + KV_W:o + 2 * KV_W].astype(v_ref.dtype)
    o += 2 * KV_W
    u_ref[...] = jax.nn.gelu(p[:, o:o + A_W]).astype(u_ref.dtype)
    vg = jax.nn.gelu(p[:, o + A_W:o + 2 * A_W])
    m64 = m64_ref[...]
    mu = _dot(vg.astype(MXU_DTYPE), m64)
    cen = vg - mu
    var = _dot((cen * cen).astype(MXU_DTYPE), m64)
    vn_ref[...] = (cen * lax.rsqrt(var + EPS)).astype(vn_ref.dtype)


def _inproj(x2, mod, mod_index, g, w_ext, cos, sin, m64, seq_tiles):
    n, d = x2.shape
    tm = TM_DENSE
    wcols = w_ext.shape[1]
    row = lambda i: (i, 0)
    const = lambda i: (0, 0)
    outs = [jax.ShapeDtypeStruct((n, QPAD_W), MXU_DTYPE),
            jax.ShapeDtypeStruct((n, KV_W), MXU_DTYPE),
            jax.ShapeDtypeStruct((n, KV_W), MXU_DTYPE),
            jax.ShapeDtypeStruct((n, A_W), MXU_DTYPE),
            jax.ShapeDtypeStruct((n, A_W), MXU_DTYPE)]
    return pl.pallas_call(
        _inproj_kernel,
        grid=(n // tm,),
        in_specs=[pl.BlockSpec((tm, d), row),
                  pl.BlockSpec((1, N_MOD, d), mod_index),
                  pl.BlockSpec((1, d), const),
                  pl.BlockSpec((d, wcols), const),
                  pl.BlockSpec((tm, LANES), lambda i: (i % seq_tiles, 0)),
                  pl.BlockSpec((tm, LANES), lambda i: (i % seq_tiles, 0)),
                  pl.BlockSpec((A_W, A_W), const)],
        out_specs=[pl.BlockSpec((tm, QPAD_W), row), pl.BlockSpec((tm, KV_W), row), pl.BlockSpec((tm, KV_W), row),
                   pl.BlockSpec((tm, A_W), row), pl.BlockSpec((tm, A_W), row)],
        out_shape=outs,
        compiler_params=pltpu.CompilerParams(vmem_limit_bytes=VMEM_LIMIT),
        name="inproj_rope",
    )(x2, mod, g, w_ext, cos, sin, m64)


def _attn_gmlp_kernel(sink_ref, q_ref, kp_ref, kc_ref, kn_ref, vp_ref, vc_ref, vn_ref, kx_ref, vx_ref,
                      u_ref, g_ref, ws_ref, bs_ref, attn_ref, mix_ref):
    n = pl.program_id(1)
    nb = pl.num_programs(1)
    rows = GQA_GROUP * ATTN_BLOCK
    ri = lax.broadcasted_iota(jnp.int32, (rows, ATTN_BLOCK), 0) & (ATTN_BLOCK - 1)
    ci = lax.broadcasted_iota(jnp.int32, (rows, ATTN_BLOCK), 1)
    mask_prev = jnp.logical_and(ci >= ri, n > 0)
    mask_next = jnp.logical_and(ci <= ri, n < nb - 1)
    rblk = lax.broadcasted_iota(jnp.int32, (rows, 1), 0) >> (ATTN_BLOCK.bit_length() - 1)
    kp, kc, kn, kx = kp_ref[...], kc_ref[...], kn_ref[...], kx_ref[...]
    vp, vc, vn, vx = vp_ref[...], vc_ref[...], vn_ref[...], vx_ref[...]
    for g in range(KV_HEADS):
        h0 = g * GQA_GROUP
        qg = jnp.concatenate([q_ref[:, (h0 + j) * LANES:(h0 + j + 1) * LANES] for j in range(GQA_GROUP)], axis=0)
        sink = jnp.full((rows, 1), sink_ref[h0], F32)
        for j in range(1, GQA_GROUP):
            sink = jnp.where(rblk == j, sink_ref[h0 + j], sink)
        s_p = jnp.where(mask_prev, _dot_nt(qg, kp), NEG)
        s_c = _dot_nt(qg, kc)
        s_n = jnp.where(mask_next, _dot_nt(qg, kn), NEG)
        s_x = _dot_nt(qg, kx)
        m = jnp.maximum(jnp.maximum(jnp.max(s_p, axis=1, keepdims=True), jnp.max(s_c, axis=1, keepdims=True)),
                        jnp.maximum(jnp.max(s_n, axis=1, keepdims=True), jnp.max(s_x, axis=1, keepdims=True)))
        m = jnp.maximum(m, sink)
        p_p, p_c, p_n, p_x = jnp.exp(s_p - m), jnp.exp(s_c - m), jnp.exp(s_n - m), jnp.exp(s_x - m)
        den = (jnp.sum(p_p, axis=1, keepdims=True) + jnp.sum(p_c, axis=1, keepdims=True)
               + jnp.sum(p_n, axis=1, keepdims=True) + jnp.sum(p_x, axis=1, keepdims=True) + jnp.exp(sink - m))
        o = (_dot(p_p.astype(MXU_DTYPE), vp) + _dot(p_c.astype(MXU_DTYPE), vc)
             + _dot(p_n.astype(MXU_DTYPE), vn) + _dot(p_x.astype(MXU_DTYPE), vx))
        o = o / den
        for j in range(GQA_GROUP):
            attn_ref[:, (h0 + j) * LANES:(h0 + j + 1) * LANES] = (
                o[j * ATTN_BLOCK:(j + 1) * ATTN_BLOCK].astype(attn_ref.dtype))
    lane = lax.broadcasted_iota(jnp.int32, (CHUNK, LANES), 1)
    for j in range(GMLP_GROUPS // 2):
        gp = g_ref[:, j * LANES:(j + 1) * LANES]
        mixed = jnp.where(lane < GMLP_DIM, _dot(ws_ref[2 * j], gp), _dot(ws_ref[2 * j + 1], gp)) + bs_ref[j]
        mix_ref[:, j * LANES:(j + 1) * LANES] = (u_ref[:, j * LANES:(j + 1) * LANES].astype(F32) * mixed
                                                 ).astype(mix_ref.dtype)


def _attn_gmlp(sink, q, k, v, kx, vx, u, gn, ws, bs, batch, seq):
    n = q.shape[0]
    nb = seq // ATTN_BLOCK
    cx = kx.shape[0] // batch
    blk = ATTN_BLOCK
    cur = lambda b, i: (b * nb + i, 0)
    prev = lambda b, i: (b * nb + jnp.maximum(i - 1, 0), 0)
    nxt = lambda b, i: (b * nb + jnp.minimum(i + 1, nb - 1), 0)
    ctx = lambda b, i: (b, 0)
    return pl.pallas_call(
        _attn_gmlp_kernel,
        grid=(batch, nb),
        in_specs=[pl.BlockSpec(memory_space=pltpu.SMEM),
                  pl.BlockSpec((blk, QPAD_W), cur),
                  pl.BlockSpec((blk, KV_W), prev), pl.BlockSpec((blk, KV_W), cur), pl.BlockSpec((blk, KV_W), nxt),
                  pl.BlockSpec((blk, KV_W), prev), pl.BlockSpec((blk, KV_W), cur), pl.BlockSpec((blk, KV_W), nxt),
                  pl.BlockSpec((cx, KV_W), ctx), pl.BlockSpec((cx, KV_W), ctx),
                  pl.BlockSpec((blk, A_W), cur), pl.BlockSpec((blk, A_W), cur),
                  pl.BlockSpec(ws.shape, lambda b, i: (0, 0, 0)),
                  pl.BlockSpec(bs.shape, lambda b, i: (0, 0, 0))],
        out_specs=[pl.BlockSpec((blk, QPAD_W), cur), pl.BlockSpec((blk, A_W), cur)],
        out_shape=[jax.ShapeDtypeStruct((n, QPAD_W), MXU_DTYPE), jax.ShapeDtypeStruct((n, A_W), MXU_DTYPE)],
        compiler_params=pltpu.CompilerParams(vmem_limit_bytes=VMEM_LIMIT),
        name="window_attn_gmlp",
    )(sink, q, k, k, k, v, v, v, kx, vx, u, gn, ws, bs)


def _residual_norm2(x, y, mod_ref, g2_ref, x1_ref, h2_ref):
    g1 = mod_ref[0, 2:3, :]
    sh2 = mod_ref[0, 3:4, :]
    sc2 = mod_ref[0, 4:5, :]
    x1 = x + g1 * y
    x1_ref[...] = x1
    h2_ref[...] = _rms(x1) * g2_ref[...] * (1.0 + sc2) + sh2


def _outproj_kernel(x_ref, a_ref, m_ref, wa_ref, wm_ref, mod_ref, g2_ref, x1_ref, h2_ref):
    y = _dot(a_ref[...], wa_ref[...]) + _dot(m_ref[...], wm_ref[...])
    _residual_norm2(x_ref[...], y, mod_ref, g2_ref, x1_ref, h2_ref)


def _outproj(x2, attn, mix, wa, wm, mod, mod_index, g2):
    n, d = x2.shape
    tm = TM_DENSE
    row = lambda i: (i, 0)
    const = lambda i: (0, 0)
    return pl.pallas_call(
        _outproj_kernel,
        grid=(n // tm,),
        in_specs=[pl.BlockSpec((tm, d), row), pl.BlockSpec((tm, QPAD_W), row), pl.BlockSpec((tm, A_W), row),
                  pl.BlockSpec(wa.shape, const), pl.BlockSpec(wm.shape, const),
                  pl.BlockSpec((1, N_MOD, d), mod_index), pl.BlockSpec((1, d), const)],
        out_specs=[pl.BlockSpec((tm, d), row), pl.BlockSpec((tm, d), row)],
        out_shape=[jax.ShapeDtypeStruct((n, d), F32), jax.ShapeDtypeStruct((n, d), F32)],
        compiler_params=pltpu.CompilerParams(vmem_limit_bytes=VMEM_LIMIT),
        name="outproj_residual",
    )(x2, attn, mix, wa, wm, mod, g2)


def _pool_kernel(xp_ref, xc_ref, xn_ref, mod_ref, g1_ref, g2_ref, wp_ref, ps_ref, x1_ref, h2_ref, hs_ref,
                 *, seq_tiles, seq):
    i = pl.program_id(0)
    tm = xc_ref.shape[0]
    sh = mod_ref[0, 0:1, :]
    sc = mod_ref[0, 1:2, :]
    g1 = g1_ref[...]

    def norm(x):
        return _rms(x) * g1 * (1.0 + sc) + sh

    it = i % seq_tiles
    x = xc_ref[...]
    hs_ref[0:POOL_HALO, :] = jnp.where(it > 0, norm(xp_ref[...]), 0.0)
    hs_ref[POOL_HALO:POOL_HALO + tm, :] = norm(x)
    hs_ref[POOL_HALO + tm:, :] = jnp.where(it < seq_tiles - 1, norm(xn_ref[...]), 0.0)
    pos = it * tm + lax.broadcasted_iota(jnp.int32, (tm, 1), 0)
    y_parts = []
    for g, w in enumerate(POOL_WINDOWS):
        cols = slice(g * POOL_DIM, (g + 1) * POOL_DIM)
        tot = hs_ref[POOL_HALO - w // 2:POOL_HALO - w // 2 + tm, cols]
        for dlt in range(-w // 2 + 1, w // 2):
            tot = tot + hs_ref[POOL_HALO + dlt:POOL_HALO + dlt + tm, cols]
        cnt = (jnp.minimum(pos + w // 2, seq) - jnp.maximum(pos - w // 2, 0)).astype(F32)
        pooled = tot / cnt - hs_ref[POOL_HALO:POOL_HALO + tm, cols]
        y_parts.append(_dot(pooled.astype(MXU_DTYPE), wp_ref[g]))
    y = jnp.concatenate(y_parts, axis=1) * ps_ref[...]
    _residual_norm2(x, y, mod_ref, g2_ref, x1_ref, h2_ref)


def _pool(x2, mod, mod_index, g1, g2, wp, ps, seq):
    n, d = x2.shape
    tm = TM_DENSE
    seq_tiles = seq // tm
    hb = tm // POOL_HALO
    nh = n // POOL_HALO
    row = lambda i: (i, 0)
    const = lambda i: (0, 0)
    return pl.pallas_call(
        functools.partial(_pool_kernel, seq_tiles=seq_tiles, seq=seq),
        grid=(n // tm,),
        in_specs=[pl.BlockSpec((POOL_HALO, d), lambda i: (jnp.maximum(i * hb - 1, 0), 0)),
                  pl.BlockSpec((tm, d), row),
                  pl.BlockSpec((POOL_HALO, d), lambda i: (jnp.minimum((i + 1) * hb, nh - 1), 0)),
                  pl.BlockSpec((1, N_MOD, d), mod_index),
                  pl.BlockSpec((1, d), const), pl.BlockSpec((1, d), const),
                  pl.BlockSpec(wp.shape, lambda i: (0, 0, 0)), pl.BlockSpec((1, d), const)],
        out_specs=[pl.BlockSpec((tm, d), row), pl.BlockSpec((tm, d), row)],
        out_shape=[jax.ShapeDtypeStruct((n, d), F32), jax.ShapeDtypeStruct((n, d), F32)],
        scratch_shapes=[pltpu.VMEM((tm + 2 * POOL_HALO, d), F32)],
        compiler_params=pltpu.CompilerParams(vmem_limit_bytes=VMEM_LIMIT),
        name="pool_mixer_residual",
    )(x2, x2, x2, mod, g1, g2, wp, ps)


def _top16(s):
    t = s.shape[1]
    rows = lax.broadcasted_iota(jnp.int32, s.shape, 0)
    r16 = lax.broadcasted_iota(jnp.int32, (PEER_TOPK, t), 0)

    def body(r, carry):
        s, tv, ti = carry
        m = jnp.max(s, axis=0, keepdims=True)
        am = jnp.min(jnp.where(s == m, rows, PEER_NKEYS), axis=0, keepdims=True)
        tv = jnp.where(r16 == r, m, tv)
        ti = jnp.where(r16 == r, am, ti)
        s = jnp.where(rows == am, -jnp.inf, s)
        return s, tv, ti

    _, tv, ti = lax.fori_loop(0, PEER_TOPK, body,
                              (s, jnp.zeros((PEER_TOPK, t), F32), jnp.zeros((PEER_TOPK, t), jnp.int32)))
    return tv, ti


def _combine16(tv1, ti1, tv2, ti2):
    k = PEER_TOPK
    t = tv1.shape[1]
    half = k // 2
    sv = [tv1[0:1] + tv2]
    ev = [ti1[0:1] * PEER_NKEYS + ti2]
    for a in range(1, half):
        sv.append(tv1[a:a + 1] + tv2[0:half])
        ev.append(ti1[a:a + 1] * PEER_NKEYS + ti2[0:half])
    sv.append(tv1[half:k] + tv2[0:1])
    ev.append(ti1[half:k] * PEER_NKEYS + ti2[0:1])
    cand = jnp.concatenate(sv, axis=0)
    cexp = jnp.concatenate(ev, axis=0)
    nrow = cand.shape[0]
    r = lax.broadcasted_iota(jnp.int32, (nrow, t), 0)
    mid = r - k
    hbits = half.bit_length() - 1
    flat = jnp.where(r < k, r,
                     jnp.where(r < k + (half - 1) * half, ((mid >> hbits) + 1) * k + (mid & (half - 1)),
                               (r - (k + (half - 1) * half) + half) * k))
    r16 = lax.broadcasted_iota(jnp.int32, (k, t), 0)

    def body(j, carry):
        cand, bs, be = carry
        m = jnp.max(cand, axis=0, keepdims=True)
        sel = jnp.min(jnp.where(cand == m, flat, k * k), axis=0, keepdims=True)
        hit = flat == sel
        e = jnp.max(jnp.where(hit, cexp, -1), axis=0, keepdims=True)
        bs = jnp.where(r16 == j, m, bs)
        be = jnp.where(r16 == j, e, be)
        cand = jnp.where(hit, -jnp.inf, cand)
        return cand, bs, be

    _, bs, be = lax.fori_loop(0, k, body, (cand, jnp.zeros((k, t), F32), jnp.zeros((k, t), jnp.int32)))
    w = jnp.exp(bs - bs[0:1])
    gate = w / jnp.sum(w, axis=0, keepdims=True)
    return gate, be


def _router_kernel(h2_ref, wq_ref, sk_ref, idx_ref, gate_ref, st_ref):
    hb = h2_ref[...].astype(MXU_DTYPE)
    qt = _dot_nt(wq_ref[...], hb)
    for hp in range(PEER_HEADS * 2):
        blk = qt[hp * PEER_DHALF:(hp + 1) * PEER_DHALF].astype(MXU_DTYPE)
        st_ref[hp] = _dot(sk_ref[hp % 2], blk)
    for h in range(PEER_HEADS):
        tv1, ti1 = _top16(st_ref[2 * h])
        tv2, ti2 = _top16(st_ref[2 * h + 1])
        gate, be = _combine16(tv1, ti1, tv2, ti2)
        idx_ref[h * PEER_TOPK:(h + 1) * PEER_TOPK, :] = be
        gate_ref[h * PEER_TOPK:(h + 1) * PEER_TOPK, :] = gate


def _router(h2, wq_t, sk):
    n, d = h2.shape
    tm = TM_DENSE
    col = lambda i: (0, i)
    return pl.pallas_call(
        _router_kernel,
        grid=(n // tm,),
        in_specs=[pl.BlockSpec((tm, d), lambda i: (i, 0)),
                  pl.BlockSpec(wq_t.shape, lambda i: (0, 0)),
                  pl.BlockSpec(sk.shape, lambda i: (0, 0, 0))],
        out_specs=[pl.BlockSpec((PEER_SEL, tm), col), pl.BlockSpec((PEER_SEL, tm), col)],
        out_shape=[jax.ShapeDtypeStruct((PEER_SEL, n), jnp.int32), jax.ShapeDtypeStruct((PEER_SEL, n), F32)],
        scratch_shapes=[pltpu.VMEM((PEER_HEADS * 2, PEER_NKEYS, tm), F32)],
        compiler_params=pltpu.CompilerParams(vmem_limit_bytes=VMEM_LIMIT),
        name="peer_router",
    )(h2, wq_t, sk)


def _gather_kernel(idx_hbm, gate_ref, h2_ref, x1_ref, mod_ref, fg_ref, tab_hbm, out_ref,
                   idx_smem, buf, isem, gsem, *, tb, final_norm):
    i = pl.program_id(0)
    nsteps = pl.num_programs(0)
    slot = i % 2
    rows = tb * PEER_SEL

    def idx_copy(step, sl):
        return pltpu.make_async_copy(idx_hbm.at[pl.ds(step * tb, tb), :], idx_smem.at[sl], isem.at[sl])

    def issue(sl):
        def tok(t, carry):
            for s in range(PEER_SEL):
                e = idx_smem[sl, t, s]
                pltpu.make_async_copy(tab_hbm.at[pl.ds(e, 1), :],
                                      buf.at[sl, pl.ds(t * PEER_SEL + s, 1), :], gsem.at[sl]).start()
            return carry
        lax.fori_loop(0, tb, tok, 0)

    @pl.when(i == 0)
    def _():
        c = idx_copy(0, 0)
        c.start()
        c.wait()
        issue(0)

        @pl.when(nsteps > 1)
        def _():
            idx_copy(1, 1).start()

    @pl.when(i + 1 < nsteps)
    def _():
        idx_copy(i + 1, 1 - slot).wait()
        issue(1 - slot)

    @pl.when(i + 2 < nsteps)
    def _():
        idx_copy(i + 2, slot).start()

    pltpu.make_async_copy(tab_hbm.at[pl.ds(0, rows), :], buf.at[slot], gsem.at[slot]).wait()

    g2 = mod_ref[0, 5:6, :]
    gate = gate_ref[...]
    lane = lax.broadcasted_iota(jnp.int32, gate.shape, 1)
    lane0 = (i % (LANES // tb)) * tb
    nchunk = D_MODEL // LANES

    sub = lax.broadcasted_iota(jnp.int32, (SUBLANES, D_MODEL), 0)

    def group(t8, carry):
        r0 = pl.multiple_of(t8 * SUBLANES, SUBLANES)
        hblk = h2_ref[pl.ds(r0, SUBLANES), :]
        yblk = jnp.zeros((SUBLANES, D_MODEL), F32)
        for j in range(SUBLANES):
            t = r0 + j
            base = pl.multiple_of(t * PEER_SEL, PEER_SEL)
            acc = jnp.zeros((PEER_SEL, LANES), F32)
            for c in range(nchunk):
                g = buf[slot, pl.ds(base, PEER_SEL), c * LANES:(c + 1) * LANES]
                dn = lax.bitcast_convert_type(g & jnp.uint32(0xFFFF0000), F32)
                acc = acc + dn * hblk[j:j + 1, c * LANES:(c + 1) * LANES]
            d = jnp.sum(acc, axis=1, keepdims=True)
            gcol = jnp.sum(jnp.where(lane == lane0 + t, gate, 0.0), axis=1, keepdims=True)
            a = gcol * jax.nn.gelu(d)
            parts = []
            for c in range(nchunk):
                g = buf[slot, pl.ds(base, PEER_SEL), c * LANES:(c + 1) * LANES]
                up = lax.bitcast_convert_type(g << 16, F32)
                parts.append(jnp.sum(a * up, axis=0, keepdims=True))
            yblk = jnp.where(sub == j, jnp.concatenate(parts, axis=1), yblk)
        x2 = x1_ref[pl.ds(r0, SUBLANES), :] + g2 * yblk
        if final_norm:
            x2 = _rms(x2) * fg_ref[...]
        out_ref[pl.ds(r0, SUBLANES), :] = x2
        return carry

    lax.fori_loop(0, tb // SUBLANES, group, 0)


def _gather(idx_tm, gate, h2, x1, mod, mod_index, fg, table, final_norm):
    n, d = x1.shape
    tb = TB_GATHER
    row = lambda i: (i, 0)
    per = LANES // tb
    return pl.pallas_call(
        functools.partial(_gather_kernel, tb=tb, final_norm=final_norm),
        grid=(n // tb,),
        in_specs=[pl.BlockSpec(memory_space=pl.ANY),
                  pl.BlockSpec((PEER_SEL, LANES), lambda i: (0, i // per)),
                  pl.BlockSpec((tb, d), row), pl.BlockSpec((tb, d), row),
                  pl.BlockSpec((1, N_MOD, d), mod_index),
                  pl.BlockSpec((1, d), lambda i: (0, 0)),
                  pl.BlockSpec(memory_space=pl.ANY)],
        out_specs=pl.BlockSpec((tb, d), row),
        out_shape=jax.ShapeDtypeStruct((n, d), F32),
        scratch_shapes=[pltpu.SMEM((2, tb, PEER_SEL), jnp.int32),
                        pltpu.VMEM((2, tb * PEER_SEL, d), jnp.uint32),
                        pltpu.SemaphoreType.DMA((2,)),
                        pltpu.SemaphoreType.DMA((2,))],
        compiler_params=pltpu.CompilerParams(dimension_semantics=("arbitrary",), vmem_limit_bytes=VMEM_LIMIT),
        name="peer_gather",
    )(idx_tm, gate, h2, x1, mod, fg, table)


def _peer(h2, x1, mod, mod_index, wq_t, sk, table, fg, final_norm):
    idx, gate = _router(h2, wq_t, sk)
    return _gather(idx.T, gate, h2, x1, mod, mod_index, fg, table, final_norm)


def _pack_table(down, up):
    hi = lax.bitcast_convert_type(down.astype(jnp.bfloat16), jnp.uint16).astype(jnp.uint32)
    lo = lax.bitcast_convert_type(up.astype(jnp.bfloat16), jnp.uint16).astype(jnp.uint32)
    return (hi << 16) | lo


def _rope_tables(seq):
    quarter = HEAD_DIM // 4
    inv = jnp.power(ROPE_BASE, -jnp.arange(quarter, dtype=F32) / quarter)
    t = jnp.arange(seq)
    row = (t // GRID_W).astype(F32)
    col = (t % GRID_W).astype(F32)
    ang_r = row[:, None] * inv[None, :]
    ang_c = col[:, None] * inv[None, :]
    cos = jnp.concatenate([jnp.cos(ang_r)] * 2 + [jnp.cos(ang_c)] * 2, axis=1)
    sin = jnp.concatenate([-jnp.sin(ang_r), jnp.sin(ang_r), -jnp.sin(ang_c), jnp.sin(ang_c)], axis=1)
    reps = LANES // HEAD_DIM
    return jnp.tile(cos, (1, reps)), jnp.tile(sin, (1, reps))


def _inproj_weights(w_in):
    d = w_in.shape[0]
    slots = []
    for h in range(ATTN_HEADS):
        g = h // GQA_GROUP
        blk = w_in[:, h * HEAD_DIM:(h + 1) * HEAD_DIM]
        z = jnp.zeros((d, HEAD_DIM), w_in.dtype)
        slots.append(jnp.concatenate([blk, z] if g == 0 else [z, blk], axis=1))
    return jnp.concatenate(slots + [w_in[:, Q_W:]], axis=1).astype(MXU_DTYPE)


def _outproj_weights(w_out):
    d = w_out.shape[1]
    slots = []
    for h in range(ATTN_HEADS):
        g = h // GQA_GROUP
        blk = w_out[h * HEAD_DIM:(h + 1) * HEAD_DIM]
        z = jnp.zeros((HEAD_DIM, d), w_out.dtype)
        slots.append(jnp.concatenate([blk, z] if g == 0 else [z, blk], axis=0))
    return jnp.concatenate(slots, axis=0).astype(MXU_DTYPE), w_out[Q_W:].astype(MXU_DTYPE)


def kernel(x, c, ctx, c_ctx, mod_w, mod_b, norm1_g, norm2_g, attn_in_w, attn_out_w, attn_sink, gmlp_w_s, gmlp_b_s, pool_w, pool_scale, peer_w_q, peer_sub_keys, peer_down, peer_up, final_g):
    batch, seq, d = x.shape
    n = batch * seq
    n_ctx = ctx.shape[1]
    depth = mod_w.shape[0]
    assert depth == 2 and d == D_MODEL
    assert seq % TM_DENSE == 0 and n_ctx % TM_DENSE == 0 and LANES % TB_GATHER == 0

    mod_rows = -(-(batch + 1) // SUBLANES) * SUBLANES
    cs = jnp.zeros((mod_rows, d), F32).at[:batch].set(c).at[batch].set(c_ctx)
    mods = _modulation(cs, mod_w, mod_b)
    seq_tiles = seq // TM_DENSE
    main_mod = lambda i: (i // seq_tiles, 0, 0)
    ctx_mod = lambda i: (batch, 0, 0)
    gather_mod = lambda i: (i // (seq // TB_GATHER), 0, 0)

    x2 = x.reshape(n, d)
    row1 = lambda v: v.reshape(1, -1)

    w_ext = _inproj_weights(attn_in_w[0])
    cos, sin = _rope_tables(seq)
    m64 = jnp.kron(jnp.eye(GMLP_GROUPS, dtype=F32), jnp.full((GMLP_DIM, GMLP_DIM), 1.0 / GMLP_DIM, F32)
                   ).astype(MXU_DTYPE)
    q, k, v, u, gn = _inproj(x2, mods[0], main_mod, row1(norm1_g[0]), w_ext, cos, sin, m64, seq_tiles)
    ones = jnp.ones((n_ctx, LANES), F32)
    _, kx, vx, _, _ = _inproj(ctx.reshape(batch * n_ctx, d), mods[0], ctx_mod, row1(norm1_g[0]), w_ext,
                              ones, jnp.zeros_like(ones), m64, n_ctx // TM_DENSE)
    ws = gmlp_w_s[0].astype(MXU_DTYPE)
    bt = gmlp_b_s[0]
    bs = jnp.stack([jnp.concatenate([jnp.broadcast_to(bt[2 * j][:, None], (CHUNK, GMLP_DIM)),
                                     jnp.broadcast_to(bt[2 * j + 1][:, None], (CHUNK, GMLP_DIM))], axis=1)
                    for j in range(GMLP_GROUPS // 2)])
    attn, mix = _attn_gmlp(attn_sink[0], q, k, v, kx, vx, u, gn, ws, bs, batch, seq)
    wa, wm = _outproj_weights(attn_out_w[0])
    x1, h2 = _outproj(x2, attn, mix, wa, wm, mods[0], main_mod, row1(norm2_g[0]))
    fg = row1(final_g)
    xl = _peer(h2, x1, mods[0], gather_mod, peer_w_q[0].T.astype(MXU_DTYPE), peer_sub_keys[0].astype(MXU_DTYPE),
               _pack_table(peer_down[0], peer_up[0]), fg, False)

    x1, h2 = _pool(xl, mods[1], main_mod, row1(norm1_g[1]), row1(norm2_g[1]), pool_w[0].astype(MXU_DTYPE),
                   row1(pool_scale[0]), seq)
    out = _peer(h2, x1, mods[1], gather_mod, peer_w_q[1].T.astype(MXU_DTYPE), peer_sub_keys[1].astype(MXU_DTYPE),
                _pack_table(peer_down[1], peer_up[1]), fg, True)
    return out.reshape(batch, seq, d)
```

```python
import functools

import jax
import jax.numpy as jnp
from jax import lax
from jax.experimental import pallas as pl
from jax.experimental.pallas import tpu as pltpu

D_MODEL = 1024
GRID_W = 64
HEAD_DIM = 64
ATTN_HEADS = 8
KV_HEADS = 2
GQA_GROUP = ATTN_HEADS // KV_HEADS
WINDOW = 128
ATTN_BLOCK = 128
ROPE_BASE = 10000.0
GMLP_GROUPS = 8
GMLP_DIM = 64
CHUNK = 128
Q_W = ATTN_HEADS * HEAD_DIM
KV_W = KV_HEADS * HEAD_DIM
A_W = GMLP_GROUPS * GMLP_DIM
POOL_WINDOWS = (2, 4, 8, 16)
POOL_GROUPS = 4
POOL_DIM = D_MODEL // POOL_GROUPS
POOL_HALO = 8
PEER_HEADS = 8
PEER_NKEYS = 128
PEER_TOPK = 16
PEER_DKEY = 256
PEER_DHALF = PEER_DKEY // 2
PEER_SEL = PEER_HEADS * PEER_TOPK
N_MOD = 6
EPS = 1e-6

LANES = 128
SUBLANES = 8
QPAD_W = ATTN_HEADS * LANES
MXU_DTYPE = jnp.bfloat16
NEG = -1e30
F32 = jnp.float32

TM_DENSE = 256
TB_GATHER = 16
VMEM_LIMIT = 48 * 1024 * 1024


def _rms(x):
    return x * lax.rsqrt(jnp.mean(x * x, axis=-1, keepdims=True) + EPS)


def _dot(a, b):
    return jnp.dot(a, b, preferred_element_type=F32)


def _dot_nt(a, b):
    return lax.dot_general(a, b, (((1,), (1,)), ((), ())), preferred_element_type=F32)


def _mod_kernel(c_ref, w_ref, b_ref, o_ref):
    c = c_ref[...]
    s = c * jax.nn.sigmoid(c)
    o_ref[0] = jnp.dot(s, w_ref[0], preferred_element_type=F32, precision=lax.Precision.HIGHEST) + b_ref[0]


def _modulation(cs, mod_w, mod_b):
    depth, d, n6 = mod_w.shape
    r = cs.shape[0]
    tn = 512
    out = pl.pallas_call(
        _mod_kernel,
        grid=(depth, n6 // tn),
        in_specs=[pl.BlockSpec((r, d), lambda l, j: (0, 0)),
                  pl.BlockSpec((1, d, tn), lambda l, j: (l, 0, j)),
                  pl.BlockSpec((1, 1, tn), lambda l, j: (l, 0, j))],
        out_specs=pl.BlockSpec((1, r, tn), lambda l, j: (l, 0, j)),
        out_shape=jax.ShapeDtypeStruct((depth, r, n6), F32),
        name="adaln_modulation",
    )(cs, mod_w, mod_b.reshape(depth, 1, n6))
    return out.reshape(depth, r, N_MOD, d)


def _rope(t, cos, sin, first_half):
    partner = jnp.where(first_half, pltpu.roll(t, LANES - 16, 1), pltpu.roll(t, 16, 1))
    return t * cos + partner * sin


def _inproj_kernel(x_ref, mod_ref, g_ref, w_ref, cos_ref, sin_ref, m64_ref,
                   q_ref, k_ref, v_ref, u_ref, vn_ref):
    x = x_ref[...]
    sh = mod_ref[0, 0:1, :]
    sc = mod_ref[0, 1:2, :]
    h = _rms(x) * g_ref[...] * (1.0 + sc) + sh
    p = _dot(h.astype(MXU_DTYPE), w_ref[...])
    cos = cos_ref[...]
    sin = sin_ref[...]
    lane = lax.broadcasted_iota(jnp.int32, cos.shape, 1)
    first_half = (lane & 31) < 16
    scale = HEAD_DIM ** -0.5
    for s in range(ATTN_HEADS):
        t = _rope(p[:, s * LANES:(s + 1) * LANES], cos, sin, first_half)
        q_ref[:, s * LANES:(s + 1) * LANES] = (t * scale).astype(q_ref.dtype)
    o = QPAD_W
    k_ref[...] = _rope(p[:, o:o + KV_W], cos, sin, first_half).astype(k_ref.dtype)
    v_ref[...] = p[:, o + KV_W:o + 2 * KV_W].astype(v_ref.dtype)
    o += 2 * KV_W
    u_ref[...] = jax.nn.gelu(p[:, o:o + A_W]).astype(u_ref.dtype)
    vg = jax.nn.gelu(p[:, o + A_W:o + 2 * A_W])
    m64 = m64_ref[...]
    mu = _dot(vg.astype(MXU_DTYPE), m64)
    cen = vg - mu
    var = _dot((cen * cen).astype(MXU_DTYPE), m64)
    vn_ref[...] = (cen * lax.rsqrt(var + EPS)).astype(vn_ref.dtype)


def _inproj(x2, mod, mod_index, g, w_ext, cos, sin, m64, seq_tiles):
    n, d = x2.shape
    tm = TM_DENSE
    wcols = w_ext.shape[1]
    row = lambda i: (i, 0)
    const = lambda i: (0, 0)
    outs = [jax.ShapeDtypeStruct((n, QPAD_W), MXU_DTYPE),
            jax.ShapeDtypeStruct((n, KV_W), MXU_DTYPE),
            jax.ShapeDtypeStruct((n, KV_W), MXU_DTYPE),
            jax.ShapeDtypeStruct((n, A_W), MXU_DTYPE),
            jax.ShapeDtypeStruct((n, A_W), MXU_DTYPE)]
    return pl.pallas_call(
        _inproj_kernel,
        grid=(n // tm,),
        in_specs=[pl.BlockSpec((tm, d), row),
                  pl.BlockSpec((1, N_MOD, d), mod_index),
                  pl.BlockSpec((1, d), const),
                  pl.BlockSpec((d, wcols), const),
                  pl.BlockSpec((tm, LANES), lambda i: (i % seq_tiles, 0)),
                  pl.BlockSpec((tm, LANES), lambda i: (i % seq_tiles, 0)),
                  pl.BlockSpec((A_W, A_W), const)],
        out_specs=[pl.BlockSpec((tm, QPAD_W), row), pl.BlockSpec((tm, KV_W), row), pl.BlockSpec((tm, KV_W), row),
                   pl.BlockSpec((tm, A_W), row), pl.BlockSpec((tm, A_W), row)],
        out_shape=outs,
        compiler_params=pltpu.CompilerParams(vmem_limit_bytes=VMEM_LIMIT),
        name="inproj_rope",
    )(x2, mod, g, w_ext, cos, sin, m64)


def _attn_gmlp_kernel(sink_ref, q_ref, kp_ref, kc_ref, kn_ref, vp_ref, vc_ref, vn_ref, kx_ref, vx_ref,
                      u_ref, g_ref, ws_ref, bs_ref, attn_ref, mix_ref):
    n = pl.program_id(1)
    nb = pl.num_programs(1)
    rows = GQA_GROUP * ATTN_BLOCK
    ri = lax.broadcasted_iota(jnp.int32, (rows, ATTN_BLOCK), 0) & (ATTN_BLOCK - 1)
    ci = lax.broadcasted_iota(jnp.int32, (rows, ATTN_BLOCK), 1)
    mask_prev = jnp.logical_and(ci >= ri, n > 0)
    mask_next = jnp.logical_and(ci <= ri, n < nb - 1)
    rblk = lax.broadcasted_iota(jnp.int32, (rows, 1), 0) >> (ATTN_BLOCK.bit_length() - 1)
    kp, kc, kn, kx = kp_ref[...], kc_ref[...], kn_ref[...], kx_ref[...]
    vp, vc, vn, vx = vp_ref[...], vc_ref[...], vn_ref[...], vx_ref[...]
    for g in range(KV_HEADS):
        h0 = g * GQA_GROUP
        qg = jnp.concatenate([q_ref[:, (h0 + j) * LANES:(h0 + j + 1) * LANES] for j in range(GQA_GROUP)], axis=0)
        sink = jnp.full((rows, 1), sink_ref[h0], F32)
        for j in range(1, GQA_GROUP):
            sink = jnp.where(rblk == j, sink_ref[h0 + j], sink)
        s_p = jnp.where(mask_prev, _dot_nt(qg, kp), NEG)
        s_c = _dot_nt(qg, kc)
        s_n = jnp.where(mask_next, _dot_nt(qg, kn), NEG)
        s_x = _dot_nt(qg, kx)
        m = jnp.maximum(jnp.maximum(jnp.max(s_p, axis=1, keepdims=True), jnp.max(s_c, axis=1, keepdims=True)),
                        jnp.maximum(jnp.max(s_n, axis=1, keepdims=True), jnp.max(s_x, axis=1, keepdims=True)))
        m = jnp.maximum(m, sink)
        p_p, p_c, p_n, p_x = jnp.exp(s_p - m), jnp.exp(s_c - m), jnp.exp(s_n - m), jnp.exp(s_x - m)
        den = (jnp.sum(p_p, axis=1, keepdims=True) + jnp.sum(p_c, axis=1, keepdims=True)
               + jnp.sum(p_n, axis=1, keepdims=True) + jnp.sum(p_x, axis=1, keepdims=True) + jnp.exp(sink - m))
        o = (_dot(p_p.astype(MXU_DTYPE), vp) + _dot(p_c.astype(MXU_DTYPE), vc)
             + _dot(p_n.astype(MXU_DTYPE), vn) + _dot(p_x.astype(MXU_DTYPE), vx))
        o = o / den
        for j in range(GQA_GROUP):
            attn_ref[:, (h0 + j) * LANES:(h0 + j + 1) * LANES] = (
                o[j * ATTN_BLOCK:(j + 1) * ATTN_BLOCK].astype(attn_ref.dtype))
    lane = lax.broadcasted_iota(jnp.int32, (CHUNK, LANES), 1)
    for j in range(GMLP_GROUPS // 2):
        gp = g_ref[:, j * LANES:(j + 1) * LANES]
        mixed = jnp.where(lane < GMLP_DIM, _dot(ws_ref[2 * j], gp), _dot(ws_ref[2 * j + 1], gp)) + bs_ref[j]
        mix_ref[:, j * LANES:(j + 1) * LANES] = (u_ref[:, j * LANES:(j + 1) * LANES].astype(F32) * mixed
                                                 ).astype(mix_ref.dtype)


def _attn_gmlp(sink, q, k, v, kx, vx, u, gn, ws, bs, batch, seq):
    n = q.shape[0]
    nb = seq // ATTN_BLOCK
    cx = kx.shape[0] // batch
    blk = ATTN_BLOCK
    cur = lambda b, i: (b * nb + i, 0)
    prev = lambda b, i: (b * nb + jnp.maximum(i - 1, 0), 0)
    nxt = lambda b, i: (b * nb + jnp.minimum(i + 1, nb - 1), 0)
    ctx = lambda b, i: (b, 0)
    return pl.pallas_call(
        _attn_gmlp_kernel,
        grid=(batch, nb),
        in_specs=[pl.BlockSpec(memory_space=pltpu.SMEM),
                  pl.BlockSpec((blk, QPAD_W), cur),
                  pl.BlockSpec((blk, KV_W), prev), pl.BlockSpec((blk, KV_W), cur), pl.BlockSpec((blk, KV_W), nxt),
                  pl.BlockSpec((blk, KV_W), prev), pl.BlockSpec((blk, KV_W), cur), pl.BlockSpec((blk, KV_W), nxt),
                  pl.BlockSpec((cx, KV_W), ctx), pl.BlockSpec((cx, KV_W), ctx),
                  pl.BlockSpec((blk, A_W), cur), pl.BlockSpec((blk, A_W), cur),
                  pl.BlockSpec(ws.shape, lambda b, i: (0, 0, 0)),
                  pl.BlockSpec(bs.shape, lambda b, i: (0, 0, 0))],
        out_specs=[pl.BlockSpec((blk, QPAD_W), cur), pl.BlockSpec((blk, A_W), cur)],
        out_shape=[jax.ShapeDtypeStruct((n, QPAD_W), MXU_DTYPE), jax.ShapeDtypeStruct((n, A_W), MXU_DTYPE)],
        compiler_params=pltpu.CompilerParams(vmem_limit_bytes=VMEM_LIMIT),
        name="window_attn_gmlp",
    )(sink, q, k, k, k, v, v, v, kx, vx, u, gn, ws, bs)


def _residual_norm2(x, y, mod_ref, g2_ref, x1_ref, h2_ref):
    g1 = mod_ref[0, 2:3, :]
    sh2 = mod_ref[0, 3:4, :]
    sc2 = mod_ref[0, 4:5, :]
    x1 = x + g1 * y
    x1_ref[...] = x1
    h2_ref[...] = _rms(x1) * g2_ref[...] * (1.0 + sc2) + sh2


def _outproj_kernel(x_ref, a_ref, m_ref, wa_ref, wm_ref, mod_ref, g2_ref, x1_ref, h2_ref):
    y = _dot(a_ref[...], wa_ref[...]) + _dot(m_ref[...], wm_ref[...])
    _residual_norm2(x_ref[...], y, mod_ref, g2_ref, x1_ref, h2_ref)


def _outproj(x2, attn, mix, wa, wm, mod, mod_index, g2):
    n, d = x2.shape
    tm = TM_DENSE
    row = lambda i: (i, 0)
    const = lambda i: (0, 0)
    return pl.pallas_call(
        _outproj_kernel,
        grid=(n // tm,),
        in_specs=[pl.BlockSpec((tm, d), row), pl.BlockSpec((tm, QPAD_W), row), pl.BlockSpec((tm, A_W), row),
                  pl.BlockSpec(wa.shape, const), pl.BlockSpec(wm.shape, const),
                  pl.BlockSpec((1, N_MOD, d), mod_index), pl.BlockSpec((1, d), const)],
        out_specs=[pl.BlockSpec((tm, d), row), pl.BlockSpec((tm, d), row)],
        out_shape=[jax.ShapeDtypeStruct((n, d), F32), jax.ShapeDtypeStruct((n, d), F32)],
        compiler_params=pltpu.CompilerParams(vmem_limit_bytes=VMEM_LIMIT),
        name="outproj_residual",
    )(x2, attn, mix, wa, wm, mod, g2)


def _pool_kernel(xp_ref, xc_ref, xn_ref, mod_ref, g1_ref, g2_ref, wp_ref, ps_ref, x1_ref, h2_ref, hs_ref,
                 *, seq_tiles, seq):
    i = pl.program_id(0)
    tm = xc_ref.shape[0]
    sh = mod_ref[0, 0:1, :]
    sc = mod_ref[0, 1:2, :]
    g1 = g1_ref[...]

    def norm(x):
        return _rms(x) * g1 * (1.0 + sc) + sh

    it = i % seq_tiles
    x = xc_ref[...]
    hs_ref[0:POOL_HALO, :] = jnp.where(it > 0, norm(xp_ref[...]), 0.0)
    hs_ref[POOL_HALO:POOL_HALO + tm, :] = norm(x)
    hs_ref[POOL_HALO + tm:, :] = jnp.where(it < seq_tiles - 1, norm(xn_ref[...]), 0.0)
    pos = it * tm + lax.broadcasted_iota(jnp.int32, (tm, 1), 0)
    y_parts = []
    for g, w in enumerate(POOL_WINDOWS):
        cols = slice(g * POOL_DIM, (g + 1) * POOL_DIM)
        tot = hs_ref[POOL_HALO - w // 2:POOL_HALO - w // 2 + tm, cols]
        for dlt in range(-w // 2 + 1, w // 2):
            tot = tot + hs_ref[POOL_HALO + dlt:POOL_HALO + dlt + tm, cols]
        cnt = (jnp.minimum(pos + w // 2, seq) - jnp.maximum(pos - w // 2, 0)).astype(F32)
        pooled = tot / cnt - hs_ref[POOL_HALO:POOL_HALO + tm, cols]
        y_parts.append(_dot(pooled.astype(MXU_DTYPE), wp_ref[g]))
    y = jnp.concatenate(y_parts, axis=1) * ps_ref[...]
    _residual_norm2(x, y, mod_ref, g2_ref, x1_ref, h2_ref)


def _pool(x2, mod, mod_index, g1, g2, wp, ps, seq):
    n, d = x2.shape
    tm = TM_DENSE
    seq_tiles = seq // tm
    hb = tm // POOL_HALO
    nh = n // POOL_HALO
    row = lambda i: (i, 0)
    const = lambda i: (0, 0)
    return pl.pallas_call(
        functools.partial(_pool_kernel, seq_tiles=seq_tiles, seq=seq),
        grid=(n // tm,),
        in_specs=[pl.BlockSpec((POOL_HALO, d), lambda i: (jnp.maximum(i * hb - 1, 0), 0)),
                  pl.BlockSpec((tm, d), row),
                  pl.BlockSpec((POOL_HALO, d), lambda i: (jnp.minimum((i + 1) * hb, nh - 1), 0)),
                  pl.BlockSpec((1, N_MOD, d), mod_index),
                  pl.BlockSpec((1, d), const), pl.BlockSpec((1, d), const),
                  pl.BlockSpec(wp.shape, lambda i: (0, 0, 0)), pl.BlockSpec((1, d), const)],
        out_specs=[pl.BlockSpec((tm, d), row), pl.BlockSpec((tm, d), row)],
        out_shape=[jax.ShapeDtypeStruct((n, d), F32), jax.ShapeDtypeStruct((n, d), F32)],
        scratch_shapes=[pltpu.VMEM((tm + 2 * POOL_HALO, d), F32)],
        compiler_params=pltpu.CompilerParams(vmem_limit_bytes=VMEM_LIMIT),
        name="pool_mixer_residual",
    )(x2, x2, x2, mod, g1, g2, wp, ps)


def _top16(s):
    t = s.shape[1]
    rows = lax.broadcasted_iota(jnp.int32, s.shape, 0)
    r16 = lax.broadcasted_iota(jnp.int32, (PEER_TOPK, t), 0)

    def body(r, carry):
        s, tv, ti = carry
        m = jnp.max(s, axis=0, keepdims=True)
        am = jnp.min(jnp.where(s == m, rows, PEER_NKEYS), axis=0, keepdims=True)
        tv = jnp.where(r16 == r, m, tv)
        ti = jnp.where(r16 == r, am, ti)
        s = jnp.where(rows == am, -jnp.inf, s)
        return s, tv, ti

    _, tv, ti = lax.fori_loop(0, PEER_TOPK, body,
                              (s, jnp.zeros((PEER_TOPK, t), F32), jnp.zeros((PEER_TOPK, t), jnp.int32)))
    return tv, ti


def _combine16(tv1, ti1, tv2, ti2):
    k = PEER_TOPK
    t = tv1.shape[1]
    half = k // 2
    sv = [tv1[0:1] + tv2]
    ev = [ti1[0:1] * PEER_NKEYS + ti2]
    for a in range(1, half):
        sv.append(tv1[a:a + 1] + tv2[0:half])
        ev.append(ti1[a:a + 1] * PEER_NKEYS + ti2[0:half])
    sv.append(tv1[half:k] + tv2[0:1])
    ev.append(ti1[half:k] * PEER_NKEYS + ti2[0:1])
    cand = jnp.concatenate(sv, axis=0)
    cexp = jnp.concatenate(ev, axis=0)
    nrow = cand.shape[0]
    r = lax.broadcasted_iota(jnp.int32, (nrow, t), 0)
    mid = r - k
    hbits = half.bit_length() - 1
    flat = jnp.where(r < k, r,
                     jnp.where(r < k + (half - 1) * half, ((mid >> hbits) + 1) * k + (mid & (half - 1)),
                               (r - (k + (half - 1) * half) + half) * k))
    r16 = lax.broadcasted_iota(jnp.int32, (k, t), 0)

    def body(j, carry):
        cand, bs, be = carry
        m = jnp.max(cand, axis=0, keepdims=True)
        sel = jnp.min(jnp.where(cand == m, flat, k * k), axis=0, keepdims=True)
        hit = flat == sel
        e = jnp.max(jnp.where(hit, cexp, -1), axis=0, keepdims=True)
        bs = jnp.where(r16 == j, m, bs)
        be = jnp.where(r16 == j, e, be)
        cand = jnp.where(hit, -jnp.inf, cand)
        return cand, bs, be

    _, bs, be = lax.fori_loop(0, k, body, (cand, jnp.zeros((k, t), F32), jnp.zeros((k, t), jnp.int32)))
    w = jnp.exp(bs - bs[0:1])
    gate = w / jnp.sum(w, axis=0, keepdims=True)
    return gate, be


def _router_kernel(h2_ref, wq_ref, sk_ref, idx_ref, gate_ref, st_ref):
    hb = h2_ref[...].astype(MXU_DTYPE)
    qt = _dot_nt(wq_ref[...], hb)
    for hp in range(PEER_HEADS * 2):
        blk = qt[hp * PEER_DHALF:(hp + 1) * PEER_DHALF].astype(MXU_DTYPE)
        st_ref[hp] = _dot(sk_ref[hp % 2], blk)
    for h in range(PEER_HEADS):
        tv1, ti1 = _top16(st_ref[2 * h])
        tv2, ti2 = _top16(st_ref[2 * h + 1])
        gate, be = _combine16(tv1, ti1, tv2, ti2)
        idx_ref[h * PEER_TOPK:(h + 1) * PEER_TOPK, :] = be
        gate_ref[h * PEER_TOPK:(h + 1) * PEER_TOPK, :] = gate


def _router(h2, wq_t, sk):
    n, d = h2.shape
    tm = TM_DENSE
    col = lambda i: (0, i)
    return pl.pallas_call(
        _router_kernel,
        grid=(n // tm,),
        in_specs=[pl.BlockSpec((tm, d), lambda i: (i, 0)),
                  pl.BlockSpec(wq_t.shape, lambda i: (0, 0)),
                  pl.BlockSpec(sk.shape, lambda i: (0, 0, 0))],
        out_specs=[pl.BlockSpec((PEER_SEL, tm), col), pl.BlockSpec((PEER_SEL, tm), col)],
        out_shape=[jax.ShapeDtypeStruct((PEER_SEL, n), jnp.int32), jax.ShapeDtypeStruct((PEER_SEL, n), F32)],
        scratch_shapes=[pltpu.VMEM((PEER_HEADS * 2, PEER_NKEYS, tm), F32)],
        compiler_params=pltpu.CompilerParams(vmem_limit_bytes=VMEM_LIMIT),
        name="peer_router",
    )(h2, wq_t, sk)


def _gather_kernel(idx_hbm, gate_ref, h2_ref, x1_ref, g2_ref, fg_ref, rep_ref, rept_ref, tab_hbm, out_ref,
                   idx_smem, buf, m_ref, isem, gsem, *, tb, final_norm):
    i = pl.program_id(0)
    nsteps = pl.num_programs(0)
    nchunk = D_MODEL // LANES
    tile_rows = 2 * nchunk
    tok_rows = PEER_SEL * tile_rows

    def idx_copy(block, sl):
        return pltpu.make_async_copy(idx_hbm.at[pl.ds(block * tb, tb), :], idx_smem.at[sl], isem.at[sl])

    def row_copy(sl, t, s, e):
        return pltpu.make_async_copy(tab_hbm.at[e],
                                     buf.at[sl, pl.ds(t * tok_rows + s * tile_rows, tile_rows), :],
                                     gsem.at[sl * tb + t])

    def issue(sl, t, sels):
        for s in sels:
            row_copy(sl, t, s, idx_smem[sl, t, s]).start(priority=s % 2)

    def wait_token(sl, t):
        for s in range(PEER_SEL):
            row_copy(sl, t, s, 0).wait()

    @pl.when(i == 0)
    def _():
        c = idx_copy(0, 0)
        c.start()
        c.wait()

        def tok(t, carry):
            issue(0, t, range(PEER_SEL))
            return carry
        lax.fori_loop(0, tb, tok, 0)
        idx_copy(1, 1).start()

    g2 = g2_ref[0]
    lane16 = lax.broadcasted_iota(jnp.int32, (nchunk, tok_rows), 1) & (tile_rows - 1)
    chunk = lax.broadcasted_iota(jnp.int32, (nchunk, tok_rows), 0)
    down_diag = (lane16 == chunk).astype(F32)
    up_diag = (lane16 == chunk + nchunk).astype(F32)
    sub = lax.broadcasted_iota(jnp.int32, (SUBLANES, LANES), 0)
    zpad = jnp.zeros((nchunk, LANES), F32)
    half = PEER_SEL // 2

    mxu_rows = 256
    nsplit = tok_rows // mxu_rows
    per_split = half // nsplit

    def tiles(sl, t, k):
        return buf[sl, t * tok_rows + k * mxu_rows:t * tok_rows + (k + 1) * mxu_rows, :]

    def phase(sl):
        for t8 in range(tb // SUBLANES):
            r0 = sl * tb + t8 * SUBLANES
            for j in range(SUBLANES):
                wait_token(sl, t8 * SUBLANES + j)
            for j in range(SUBLANES):
                t = t8 * SUBLANES + j
                x16 = jnp.concatenate([h2_ref[r0 + j], zpad], axis=0).astype(MXU_DTYPE)
                for k in range(nsplit):
                    cols = slice(k * mxu_rows, (k + 1) * mxu_rows)
                    part = _dot_nt(x16, tiles(sl, t, k))[0:nchunk] * down_diag[:, cols]
                    m_ref[j * nchunk:(j + 1) * nchunk, cols] = part
                    issue(1 - sl, t, range(k * per_split, (k + 1) * per_split))
            m_all = m_ref[...]
            m_hi = m_all.astype(MXU_DTYPE)
            m_lo = (m_all - m_hi.astype(F32)).astype(MXU_DTYPE)
            e = _dot(m_hi, rept_ref[...]) + _dot(m_lo, rept_ref[...])
            d = jnp.zeros((SUBLANES, PEER_SEL), F32)
            for j in range(SUBLANES):
                d = jnp.where(sub == j, jnp.sum(e[j * nchunk:(j + 1) * nchunk], axis=0, keepdims=True), d)
            a = gate_ref[r0:r0 + SUBLANES, :] * jax.nn.gelu(d)
            a_exp = _dot(a.astype(MXU_DTYPE), rep_ref[...])
            for j in range(SUBLANES):
                t = t8 * SUBLANES + j
                v = jnp.broadcast_to(a_exp[j:j + 1], (nchunk, tok_rows)) * up_diag
                v16 = jnp.concatenate([v, jnp.zeros_like(v)], axis=0).astype(MXU_DTYPE)
                y = jnp.zeros((tile_rows, LANES), F32)
                for k in range(nsplit):
                    y = y + _dot(v16[:, k * mxu_rows:(k + 1) * mxu_rows], tiles(sl, t, k))
                    issue(1 - sl, t, range(half + k * per_split, half + (k + 1) * per_split))
                x2 = x1_ref[r0 + j] + g2 * y[0:nchunk]
                if final_norm:
                    x2 = x2 * lax.rsqrt(jnp.mean(x2 * x2, keepdims=True) + EPS) * fg_ref[...]
                out_ref[r0 + j] = x2

    more = i + 1 < nsteps
    idx_copy(2 * i + 1, 1).wait()

    @pl.when(more)
    def _():
        idx_copy(2 * i + 2, 0).start()

    phase(0)

    @pl.when(more)
    def _():
        idx_copy(2 * i + 2, 0).wait()
        idx_copy(2 * i + 3, 1).start()

    phase(1)

    @pl.when(jnp.logical_not(more))
    def _():
        for t in range(tb):
            wait_token(0, t)


def _gather(idx_tm, gate_tm, h2, x1, g2, g2_index, fg, table, final_norm):
    n, nchunk, _ = x1.shape
    tb = TB_GATHER
    step = 2 * tb
    tile_rows = 2 * nchunk
    tok_rows = PEER_SEL * tile_rows
    rep = (jnp.arange(tok_rows)[None, :] // tile_rows == jnp.arange(PEER_SEL)[:, None]).astype(MXU_DTYPE)
    row3 = lambda i: (i, 0, 0)
    const = lambda i: (0, 0)
    return pl.pallas_call(
        functools.partial(_gather_kernel, tb=tb, final_norm=final_norm),
        grid=(n // step,),
        in_specs=[pl.BlockSpec(memory_space=pl.ANY),
                  pl.BlockSpec((step, PEER_SEL), lambda i: (i, 0)),
                  pl.BlockSpec((step, nchunk, LANES), row3), pl.BlockSpec((step, nchunk, LANES), row3),
                  pl.BlockSpec((1, nchunk, LANES), g2_index),
                  pl.BlockSpec((nchunk, LANES), const),
                  pl.BlockSpec((PEER_SEL, tok_rows), const),
                  pl.BlockSpec((tok_rows, PEER_SEL), const),
                  pl.BlockSpec(memory_space=pl.ANY)],
        out_specs=pl.BlockSpec((step, nchunk, LANES), row3),
        out_shape=jax.ShapeDtypeStruct((n, nchunk, LANES), F32),
        scratch_shapes=[pltpu.SMEM((2, tb, PEER_SEL), jnp.int32),
                        pltpu.VMEM((2, tb * tok_rows, LANES), MXU_DTYPE),
                        pltpu.VMEM((SUBLANES * nchunk, tok_rows), F32),
                        pltpu.SemaphoreType.DMA((2,)),
                        pltpu.SemaphoreType.DMA((2 * tb,))],
        compiler_params=pltpu.CompilerParams(dimension_semantics=("arbitrary",), vmem_limit_bytes=VMEM_LIMIT),
        name="peer_gather",
    )(idx_tm, gate_tm, h2, x1, g2, fg, rep, rep.T, table)


def _peer(h2, x1, g2, g2_index, wq_t, sk, table, fg, final_norm):
    n, d = h2.shape
    idx, gate = _router(h2, wq_t, sk)
    tile = (n, d // LANES, LANES)
    out = _gather(idx.T, gate.T, h2.reshape(tile), x1.reshape(tile), g2, g2_index, fg, table, final_norm)
    return out.reshape(n, d)


def _pack_table(down, up):
    e, d = down.shape
    chunks = (e, d // LANES, LANES)
    return jnp.concatenate([down.reshape(chunks), up.reshape(chunks)], axis=1).astype(MXU_DTYPE)


def _rope_tables(seq):
    quarter = HEAD_DIM // 4
    inv = jnp.power(ROPE_BASE, -jnp.arange(quarter, dtype=F32) / quarter)
    t = jnp.arange(seq)
    row = (t // GRID_W).astype(F32)
    col = (t % GRID_W).astype(F32)
    ang_r = row[:, None] * inv[None, :]
    ang_c = col[:, None] * inv[None, :]
    cos = jnp.concatenate([jnp.cos(ang_r)] * 2 + [jnp.cos(ang_c)] * 2, axis=1)
    sin = jnp.concatenate([-jnp.sin(ang_r), jnp.sin(ang_r), -jnp.sin(ang_c), jnp.sin(ang_c)], axis=1)
    reps = LANES // HEAD_DIM
    return jnp.tile(cos, (1, reps)), jnp.tile(sin, (1, reps))


def _inproj_weights(w_in):
    d = w_in.shape[0]
    slots = []
    for h in range(ATTN_HEADS):
        g = h // GQA_GROUP
        blk = w_in[:, h * HEAD_DIM:(h + 1) * HEAD_DIM]
        z = jnp.zeros((d, HEAD_DIM), w_in.dtype)
        slots.append(jnp.concatenate([blk, z] if g == 0 else [z, blk], axis=1))
    return jnp.concatenate(slots + [w_in[:, Q_W:]], axis=1).astype(MXU_DTYPE)


def _outproj_weights(w_out):
    d = w_out.shape[1]
    slots = []
    for h in range(ATTN_HEADS):
        g = h // GQA_GROUP
        blk = w_out[h * HEAD_DIM:(h + 1) * HEAD_DIM]
        z = jnp.zeros((HEAD_DIM, d), w_out.dtype)
        slots.append(jnp.concatenate([blk, z] if g == 0 else [z, blk], axis=0))
    return jnp.concatenate(slots, axis=0).astype(MXU_DTYPE), w_out[Q_W:].astype(MXU_DTYPE)


def kernel(x, c, ctx, c_ctx, mod_w, mod_b, norm1_g, norm2_g, attn_in_w, attn_out_w, attn_sink, gmlp_w_s, gmlp_b_s, pool_w, pool_scale, peer_w_q, peer_sub_keys, peer_down, peer_up, final_g):
    batch, seq, d = x.shape
    n = batch * seq
    n_ctx = ctx.shape[1]
    depth = mod_w.shape[0]
    assert depth == 2 and d == D_MODEL
    assert seq % TM_DENSE == 0 and n_ctx % TM_DENSE == 0 and LANES % (2 * TB_GATHER) == 0

    mod_rows = -(-(batch + 1) // SUBLANES) * SUBLANES
    cs = jnp.zeros((mod_rows, d), F32).at[:batch].set(c).at[batch].set(c_ctx)
    mods = _modulation(cs, mod_w, mod_b)
    seq_tiles = seq // TM_DENSE
    main_mod = lambda i: (i // seq_tiles, 0, 0)
    ctx_mod = lambda i: (batch, 0, 0)
    gather_mod = lambda i: (i // (seq // (2 * TB_GATHER)), 0, 0)

    x2 = x.reshape(n, d)
    row1 = lambda v: v.reshape(1, -1)

    w_ext = _inproj_weights(attn_in_w[0])
    cos, sin = _rope_tables(seq)
    m64 = jnp.kron(jnp.eye(GMLP_GROUPS, dtype=F32), jnp.full((GMLP_DIM, GMLP_DIM), 1.0 / GMLP_DIM, F32)
                   ).astype(MXU_DTYPE)
    q, k, v, u, gn = _inproj(x2, mods[0], main_mod, row1(norm1_g[0]), w_ext, cos, sin, m64, seq_tiles)
    ones = jnp.ones((n_ctx, LANES), F32)
    _, kx, vx, _, _ = _inproj(ctx.reshape(batch * n_ctx, d), mods[0], ctx_mod, row1(norm1_g[0]), w_ext,
                              ones, jnp.zeros_like(ones), m64, n_ctx // TM_DENSE)
    ws = gmlp_w_s[0].astype(MXU_DTYPE)
    bt = gmlp_b_s[0]
    bs = jnp.stack([jnp.concatenate([jnp.broadcast_to(bt[2 * j][:, None], (CHUNK, GMLP_DIM)),
                                     jnp.broadcast_to(bt[2 * j + 1][:, None], (CHUNK, GMLP_DIM))], axis=1)
                    for j in range(GMLP_GROUPS // 2)])
    attn, mix = _attn_gmlp(attn_sink[0], q, k, v, kx, vx, u, gn, ws, bs, batch, seq)
    wa, wm = _outproj_weights(attn_out_w[0])
    x1, h2 = _outproj(x2, attn, mix, wa, wm, mods[0], main_mod, row1(norm2_g[0]))
    chunked = lambda v: v.reshape(v.shape[:-1] + (d // LANES, LANES))
    fg = chunked(final_g)
    xl = _peer(h2, x1, chunked(mods[0][:, 5]), gather_mod, peer_w_q[0].T.astype(MXU_DTYPE),
               peer_sub_keys[0].astype(MXU_DTYPE), _pack_table(peer_down[0], peer_up[0]), fg, False)

    x1, h2 = _pool(xl, mods[1], main_mod, row1(norm1_g[1]), row1(norm2_g[1]), pool_w[0].astype(MXU_DTYPE),
                   row1(pool_scale[0]), seq)
    out = _peer(h2, x1, chunked(mods[1][:, 5]), gather_mod, peer_w_q[1].T.astype(MXU_DTYPE),
                peer_sub_keys[1].astype(MXU_DTYPE), _pack_table(peer_down[1], peer_up[1]), fg, True)
    return out.reshape(batch, seq, d)
```

```python
import functools

import jax
import jax.numpy as jnp
from jax import lax
from jax.experimental import pallas as pl
from jax.experimental.pallas import tpu as pltpu

D_MODEL = 1024
GRID_W = 64
HEAD_DIM = 64
ATTN_HEADS = 8
KV_HEADS = 2
GQA_GROUP = ATTN_HEADS // KV_HEADS
WINDOW = 128
ATTN_BLOCK = 128
ROPE_BASE = 10000.0
GMLP_GROUPS = 8
GMLP_DIM = 64
CHUNK = 128
Q_W = ATTN_HEADS * HEAD_DIM
KV_W = KV_HEADS * HEAD_DIM
A_W = GMLP_GROUPS * GMLP_DIM
POOL_WINDOWS = (2, 4, 8, 16)
POOL_GROUPS = 4
POOL_DIM = D_MODEL // POOL_GROUPS
POOL_HALO = 8
PEER_HEADS = 8
PEER_NKEYS = 128
PEER_TOPK = 16
PEER_DKEY = 256
PEER_DHALF = PEER_DKEY // 2
PEER_SEL = PEER_HEADS * PEER_TOPK
N_MOD = 6
EPS = 1e-6

LANES = 128
SUBLANES = 8
QPAD_W = ATTN_HEADS * LANES
MXU_DTYPE = jnp.bfloat16
NEG = -1e30
F32 = jnp.float32

TM_DENSE = 256
TB_GATHER = 16
VMEM_LIMIT = 48 * 1024 * 1024


def _rms(x):
    return x * lax.rsqrt(jnp.mean(x * x, axis=-1, keepdims=True) + EPS)


def _dot(a, b):
    return jnp.dot(a, b, preferred_element_type=F32)


def _dot_nt(a, b):
    return lax.dot_general(a, b, (((1,), (1,)), ((), ())), preferred_element_type=F32)


def _mod_kernel(c_ref, w_ref, b_ref, o_ref):
    c = c_ref[...]
    s = c * jax.nn.sigmoid(c)
    o_ref[0] = jnp.dot(s, w_ref[0], preferred_element_type=F32, precision=lax.Precision.HIGHEST) + b_ref[0]


def _modulation(cs, mod_w, mod_b):
    depth, d, n6 = mod_w.shape
    r = cs.shape[0]
    tn = 512
    out = pl.pallas_call(
        _mod_kernel,
        grid=(depth, n6 // tn),
        in_specs=[pl.BlockSpec((r, d), lambda l, j: (0, 0)),
                  pl.BlockSpec((1, d, tn), lambda l, j: (l, 0, j)),
                  pl.BlockSpec((1, 1, tn), lambda l, j: (l, 0, j))],
        out_specs=pl.BlockSpec((1, r, tn), lambda l, j: (l, 0, j)),
        out_shape=jax.ShapeDtypeStruct((depth, r, n6), F32),
        name="adaln_modulation",
    )(cs, mod_w, mod_b.reshape(depth, 1, n6))
    return out.reshape(depth, r, N_MOD, d)


def _rope(t, cos, sin, first_half):
    partner = jnp.where(first_half, pltpu.roll(t, LANES - 16, 1), pltpu.roll(t, 16, 1))
    return t * cos + partner * sin


def _inproj_kernel(x_ref, mod_ref, g_ref, w_ref, cos_ref, sin_ref, m64_ref,
                   q_ref, k_ref, v_ref, u_ref, vn_ref):
    x = x_ref[...]
    sh = mod_ref[0, 0:1, :]
    sc = mod_ref[0, 1:2, :]
    h = _rms(x) * g_ref[...] * (1.0 + sc) + sh
    p = _dot(h.astype(MXU_DTYPE), w_ref[...])
    cos = cos_ref[...]
    sin = sin_ref[...]
    lane = lax.broadcasted_iota(jnp.int32, cos.shape, 1)
    first_half = (lane & 31) < 16
    scale = HEAD_DIM ** -0.5
    for s in range(ATTN_HEADS):
        t = _rope(p[:, s * LANES:(s + 1) * LANES], cos, sin, first_half)
        q_ref[:, s * LANES:(s + 1) * LANES] = (t * scale).astype(q_ref.dtype)
    o = QPAD_W
    k_ref[...] = _rope(p[:, o:o + KV_W], cos, sin, first_half).astype(k_ref.dtype)
    v_ref[...] = p[:, o + KV_W:o + 2 * KV_W].astype(v_ref.dtype)
    o += 2 * KV_W
    u_ref[...] = jax.nn.gelu(p[:, o:o + A_W]).astype(u_ref.dtype)
    vg = jax.nn.gelu(p[:, o + A_W:o + 2 * A_W])
    m64 = m64_ref[...]
    mu = _dot(vg.astype(MXU_DTYPE), m64)
    cen = vg - mu
    var = _dot((cen * cen).astype(MXU_DTYPE), m64)
    vn_ref[...] = (cen * lax.rsqrt(var + EPS)).astype(vn_ref.dtype)


def _inproj(x2, mod, mod_index, g, w_ext, cos, sin, m64, seq_tiles):
    n, d = x2.shape
    tm = TM_DENSE
    wcols = w_ext.shape[1]
    row = lambda i: (i, 0)
    const = lambda i: (0, 0)
    outs = [jax.ShapeDtypeStruct((n, QPAD_W), MXU_DTYPE),
            jax.ShapeDtypeStruct((n, KV_W), MXU_DTYPE),
            jax.ShapeDtypeStruct((n, KV_W), MXU_DTYPE),
            jax.ShapeDtypeStruct((n, A_W), MXU_DTYPE),
            jax.ShapeDtypeStruct((n, A_W), MXU_DTYPE)]
    return pl.pallas_call(
        _inproj_kernel,
        grid=(n // tm,),
        in_specs=[pl.BlockSpec((tm, d), row),
                  pl.BlockSpec((1, N_MOD, d), mod_index),
                  pl.BlockSpec((1, d), const),
                  pl.BlockSpec((d, wcols), const),
                  pl.BlockSpec((tm, LANES), lambda i: (i % seq_tiles, 0)),
                  pl.BlockSpec((tm, LANES), lambda i: (i % seq_tiles, 0)),
                  pl.BlockSpec((A_W, A_W), const)],
        out_specs=[pl.BlockSpec((tm, QPAD_W), row), pl.BlockSpec((tm, KV_W), row), pl.BlockSpec((tm, KV_W), row),
                   pl.BlockSpec((tm, A_W), row), pl.BlockSpec((tm, A_W), row)],
        out_shape=outs,
        compiler_params=pltpu.CompilerParams(vmem_limit_bytes=VMEM_LIMIT),
        name="inproj_rope",
    )(x2, mod, g, w_ext, cos, sin, m64)


def _attn_gmlp_kernel(sink_ref, q_ref, kp_ref, kc_ref, kn_ref, vp_ref, vc_ref, vn_ref, kx_ref, vx_ref,
                      u_ref, g_ref, ws_ref, bs_ref, attn_ref, mix_ref):
    n = pl.program_id(1)
    nb = pl.num_programs(1)
    rows = GQA_GROUP * ATTN_BLOCK
    ri = lax.broadcasted_iota(jnp.int32, (rows, ATTN_BLOCK), 0) & (ATTN_BLOCK - 1)
    ci = lax.broadcasted_iota(jnp.int32, (rows, ATTN_BLOCK), 1)
    mask_prev = jnp.logical_and(ci >= ri, n > 0)
    mask_next = jnp.logical_and(ci <= ri, n < nb - 1)
    rblk = lax.broadcasted_iota(jnp.int32, (rows, 1), 0) >> (ATTN_BLOCK.bit_length() - 1)
    kp, kc, kn, kx = kp_ref[...], kc_ref[...], kn_ref[...], kx_ref[...]
    vp, vc, vn, vx = vp_ref[...], vc_ref[...], vn_ref[...], vx_ref[...]
    for g in range(KV_HEADS):
        h0 = g * GQA_GROUP
        qg = jnp.concatenate([q_ref[:, (h0 + j) * LANES:(h0 + j + 1) * LANES] for j in range(GQA_GROUP)], axis=0)
        sink = jnp.full((rows, 1), sink_ref[h0], F32)
        for j in range(1, GQA_GROUP):
            sink = jnp.where(rblk == j, sink_ref[h0 + j], sink)
        s_p = jnp.where(mask_prev, _dot_nt(qg, kp), NEG)
        s_c = _dot_nt(qg, kc)
        s_n = jnp.where(mask_next, _dot_nt(qg, kn), NEG)
        s_x = _dot_nt(qg, kx)
        m = jnp.maximum(jnp.maximum(jnp.max(s_p, axis=1, keepdims=True), jnp.max(s_c, axis=1, keepdims=True)),
                        jnp.maximum(jnp.max(s_n, axis=1, keepdims=True), jnp.max(s_x, axis=1, keepdims=True)))
        m = jnp.maximum(m, sink)
        p_p, p_c, p_n, p_x = jnp.exp(s_p - m), jnp.exp(s_c - m), jnp.exp(s_n - m), jnp.exp(s_x - m)
        den = (jnp.sum(p_p, axis=1, keepdims=True) + jnp.sum(p_c, axis=1, keepdims=True)
               + jnp.sum(p_n, axis=1, keepdims=True) + jnp.sum(p_x, axis=1, keepdims=True) + jnp.exp(sink - m))
        o = (_dot(p_p.astype(MXU_DTYPE), vp) + _dot(p_c.astype(MXU_DTYPE), vc)
             + _dot(p_n.astype(MXU_DTYPE), vn) + _dot(p_x.astype(MXU_DTYPE), vx))
        o = o / den
        for j in range(GQA_GROUP):
            attn_ref[:, (h0 + j) * LANES:(h0 + j + 1) * LANES] = (
                o[j * ATTN_BLOCK:(j + 1) * ATTN_BLOCK].astype(attn_ref.dtype))
    lane = lax.broadcasted_iota(jnp.int32, (CHUNK, LANES), 1)
    for j in range(GMLP_GROUPS // 2):
        gp = g_ref[:, j * LANES:(j + 1) * LANES]
        mixed = jnp.where(lane < GMLP_DIM, _dot(ws_ref[2 * j], gp), _dot(ws_ref[2 * j + 1], gp)) + bs_ref[j]
        mix_ref[:, j * LANES:(j + 1) * LANES] = (u_ref[:, j * LANES:(j + 1) * LANES].astype(F32) * mixed
                                                 ).astype(mix_ref.dtype)


def _attn_gmlp(sink, q, k, v, kx, vx, u, gn, ws, bs, batch, seq):
    n = q.shape[0]
    nb = seq // ATTN_BLOCK
    cx = kx.shape[0] // batch
    blk = ATTN_BLOCK
    cur = lambda b, i: (b * nb + i, 0)
    prev = lambda b, i: (b * nb + jnp.maximum(i - 1, 0), 0)
    nxt = lambda b, i: (b * nb + jnp.minimum(i + 1, nb - 1), 0)
    ctx = lambda b, i: (b, 0)
    return pl.pallas_call(
        _attn_gmlp_kernel,
        grid=(batch, nb),
        in_specs=[pl.BlockSpec(memory_space=pltpu.SMEM),
                  pl.BlockSpec((blk, QPAD_W), cur),
                  pl.BlockSpec((blk, KV_W), prev), pl.BlockSpec((blk, KV_W), cur), pl.BlockSpec((blk, KV_W), nxt),
                  pl.BlockSpec((blk, KV_W), prev), pl.BlockSpec((blk, KV_W), cur), pl.BlockSpec((blk, KV_W), nxt),
                  pl.BlockSpec((cx, KV_W), ctx), pl.BlockSpec((cx, KV_W), ctx),
                  pl.BlockSpec((blk, A_W), cur), pl.BlockSpec((blk, A_W), cur),
                  pl.BlockSpec(ws.shape, lambda b, i: (0, 0, 0)),
                  pl.BlockSpec(bs.shape, lambda b, i: (0, 0, 0))],
        out_specs=[pl.BlockSpec((blk, QPAD_W), cur), pl.BlockSpec((blk, A_W), cur)],
        out_shape=[jax.ShapeDtypeStruct((n, QPAD_W), MXU_DTYPE), jax.ShapeDtypeStruct((n, A_W), MXU_DTYPE)],
        compiler_params=pltpu.CompilerParams(vmem_limit_bytes=VMEM_LIMIT),
        name="window_attn_gmlp",
    )(sink, q, k, k, k, v, v, v, kx, vx, u, gn, ws, bs)


def _residual_norm2(x, y, mod_ref, g2_ref, x1c_ref, h2_ref, h2c_ref):
    g1 = mod_ref[0, 2:3, :]
    sh2 = mod_ref[0, 3:4, :]
    sc2 = mod_ref[0, 4:5, :]
    x1 = x + g1 * y
    h2 = _rms(x1) * g2_ref[...] * (1.0 + sc2) + sh2
    h2_ref[...] = h2
    for c in range(x1.shape[1] // LANES):
        x1c_ref[:, c, :] = x1[:, c * LANES:(c + 1) * LANES]
        h2c_ref[:, c, :] = h2[:, c * LANES:(c + 1) * LANES]


def _outproj_kernel(x_ref, a_ref, m_ref, wa_ref, wm_ref, mod_ref, g2_ref, x1c_ref, h2_ref, h2c_ref):
    y = _dot(a_ref[...], wa_ref[...]) + _dot(m_ref[...], wm_ref[...])
    _residual_norm2(x_ref[...], y, mod_ref, g2_ref, x1c_ref, h2_ref, h2c_ref)


def _mixer_outputs(n, d, tm):
    chunk3 = (n, d // LANES, LANES)
    specs = [pl.BlockSpec((tm, d // LANES, LANES), lambda i: (i, 0, 0)),
             pl.BlockSpec((tm, d), lambda i: (i, 0)),
             pl.BlockSpec((tm, d // LANES, LANES), lambda i: (i, 0, 0))]
    shapes = [jax.ShapeDtypeStruct(chunk3, F32), jax.ShapeDtypeStruct((n, d), F32), jax.ShapeDtypeStruct(chunk3, F32)]
    return specs, shapes


def _outproj(x2, attn, mix, wa, wm, mod, mod_index, g2):
    n, d = x2.shape
    tm = TM_DENSE
    row = lambda i: (i, 0)
    const = lambda i: (0, 0)
    out_specs, out_shape = _mixer_outputs(n, d, tm)
    return pl.pallas_call(
        _outproj_kernel,
        grid=(n // tm,),
        in_specs=[pl.BlockSpec((tm, d), row), pl.BlockSpec((tm, QPAD_W), row), pl.BlockSpec((tm, A_W), row),
                  pl.BlockSpec(wa.shape, const), pl.BlockSpec(wm.shape, const),
                  pl.BlockSpec((1, N_MOD, d), mod_index), pl.BlockSpec((1, d), const)],
        out_specs=out_specs,
        out_shape=out_shape,
        compiler_params=pltpu.CompilerParams(vmem_limit_bytes=VMEM_LIMIT),
        name="outproj_residual",
    )(x2, attn, mix, wa, wm, mod, g2)


def _pool_kernel(xp_ref, xc_ref, xn_ref, mod_ref, g1_ref, g2_ref, wp_ref, ps_ref, x1c_ref, h2_ref, h2c_ref, hs_ref,
                 *, seq_tiles, seq):
    i = pl.program_id(0)
    tm = xc_ref.shape[0]
    sh = mod_ref[0, 0:1, :]
    sc = mod_ref[0, 1:2, :]
    g1 = g1_ref[...]

    def norm(x):
        return _rms(x) * g1 * (1.0 + sc) + sh

    it = i % seq_tiles
    x = xc_ref[...]
    hs_ref[0:POOL_HALO, :] = jnp.where(it > 0, norm(xp_ref[...]), 0.0)
    hs_ref[POOL_HALO:POOL_HALO + tm, :] = norm(x)
    hs_ref[POOL_HALO + tm:, :] = jnp.where(it < seq_tiles - 1, norm(xn_ref[...]), 0.0)
    pos = it * tm + lax.broadcasted_iota(jnp.int32, (tm, 1), 0)
    y_parts = []
    for g, w in enumerate(POOL_WINDOWS):
        cols = slice(g * POOL_DIM, (g + 1) * POOL_DIM)
        tot = hs_ref[POOL_HALO - w // 2:POOL_HALO - w // 2 + tm, cols]
        for dlt in range(-w // 2 + 1, w // 2):
            tot = tot + hs_ref[POOL_HALO + dlt:POOL_HALO + dlt + tm, cols]
        cnt = (jnp.minimum(pos + w // 2, seq) - jnp.maximum(pos - w // 2, 0)).astype(F32)
        pooled = tot / cnt - hs_ref[POOL_HALO:POOL_HALO + tm, cols]
        y_parts.append(_dot(pooled.astype(MXU_DTYPE), wp_ref[g]))
    y = jnp.concatenate(y_parts, axis=1) * ps_ref[...]
    _residual_norm2(x, y, mod_ref, g2_ref, x1c_ref, h2_ref, h2c_ref)


def _pool(x2, mod, mod_index, g1, g2, wp, ps, seq):
    n, d = x2.shape
    tm = TM_DENSE
    seq_tiles = seq // tm
    hb = tm // POOL_HALO
    nh = n // POOL_HALO
    row = lambda i: (i, 0)
    const = lambda i: (0, 0)
    return pl.pallas_call(
        functools.partial(_pool_kernel, seq_tiles=seq_tiles, seq=seq),
        grid=(n // tm,),
        in_specs=[pl.BlockSpec((POOL_HALO, d), lambda i: (jnp.maximum(i * hb - 1, 0), 0)),
                  pl.BlockSpec((tm, d), row),
                  pl.BlockSpec((POOL_HALO, d), lambda i: (jnp.minimum((i + 1) * hb, nh - 1), 0)),
                  pl.BlockSpec((1, N_MOD, d), mod_index),
                  pl.BlockSpec((1, d), const), pl.BlockSpec((1, d), const),
                  pl.BlockSpec(wp.shape, lambda i: (0, 0, 0)), pl.BlockSpec((1, d), const)],
        out_specs=_mixer_outputs(n, d, tm)[0],
        out_shape=_mixer_outputs(n, d, tm)[1],
        scratch_shapes=[pltpu.VMEM((tm + 2 * POOL_HALO, d), F32)],
        compiler_params=pltpu.CompilerParams(vmem_limit_bytes=VMEM_LIMIT),
        name="pool_mixer_residual",
    )(x2, x2, x2, mod, g1, g2, wp, ps)


def _top16(s):
    t = s.shape[1]
    rows = lax.broadcasted_iota(jnp.int32, s.shape, 0).astype(F32)
    r16 = lax.broadcasted_iota(jnp.int32, (PEER_TOPK, t), 0)

    def body(r, carry):
        s, tv, ti = carry
        m = jnp.max(s, axis=0, keepdims=True)
        am = jnp.min(jnp.where(s == m, rows, float(PEER_NKEYS)), axis=0, keepdims=True)
        tv = jnp.where(r16 == r, m, tv)
        ti = jnp.where(r16 == r, am, ti)
        s = jnp.where(rows == am, -jnp.inf, s)
        return s, tv, ti

    _, tv, ti = lax.fori_loop(0, PEER_TOPK, body,
                              (s, jnp.zeros((PEER_TOPK, t), F32), jnp.zeros((PEER_TOPK, t), F32)))
    return tv, ti


def _combine16(tv1, ti1, tv2, ti2):
    k = PEER_TOPK
    t = tv1.shape[1]
    half = k // 2
    sv = [tv1[0:1] + tv2]
    ev = [ti1[0:1] * PEER_NKEYS + ti2]
    for a in range(1, half):
        sv.append(tv1[a:a + 1] + tv2[0:half])
        ev.append(ti1[a:a + 1] * PEER_NKEYS + ti2[0:half])
    sv.append(tv1[half:k] + tv2[0:1])
    ev.append(ti1[half:k] * PEER_NKEYS + ti2[0:1])
    cand = jnp.concatenate(sv, axis=0)
    cexp = jnp.concatenate(ev, axis=0)
    nrow = cand.shape[0]
    r = lax.broadcasted_iota(jnp.int32, (nrow, t), 0)
    mid = r - k
    hbits = half.bit_length() - 1
    flat = jnp.where(r < k, r,
                     jnp.where(r < k + (half - 1) * half, ((mid >> hbits) + 1) * k + (mid & (half - 1)),
                               (r - (k + (half - 1) * half) + half) * k)).astype(F32)
    r16 = lax.broadcasted_iota(jnp.int32, (k, t), 0)

    def body(j, carry):
        cand, bs, be = carry
        m = jnp.max(cand, axis=0, keepdims=True)
        sel = jnp.min(jnp.where(cand == m, flat, float(k * k)), axis=0, keepdims=True)
        hit = flat == sel
        e = jnp.max(jnp.where(hit, cexp, -1.0), axis=0, keepdims=True)
        bs = jnp.where(r16 == j, m, bs)
        be = jnp.where(r16 == j, e, be)
        cand = jnp.where(hit, -jnp.inf, cand)
        return cand, bs, be

    _, bs, be = lax.fori_loop(0, k, body, (cand, jnp.zeros((k, t), F32), jnp.zeros((k, t), F32)))
    w = jnp.exp(bs - bs[0:1])
    gate = w / jnp.sum(w, axis=0, keepdims=True)
    return gate, be.astype(jnp.int32)


def _router_kernel(h2_ref, wq_ref, sk_ref, idx_ref, gate_ref, st_ref):
    hb = h2_ref[...].astype(MXU_DTYPE)
    qt = _dot_nt(wq_ref[...], hb)
    for hp in range(PEER_HEADS * 2):
        blk = qt[hp * PEER_DHALF:(hp + 1) * PEER_DHALF].astype(MXU_DTYPE)
        st_ref[hp] = _dot(sk_ref[hp % 2], blk)
    for h in range(PEER_HEADS):
        tv1, ti1 = _top16(st_ref[2 * h])
        tv2, ti2 = _top16(st_ref[2 * h + 1])
        gate, be = _combine16(tv1, ti1, tv2, ti2)
        idx_ref[h * PEER_TOPK:(h + 1) * PEER_TOPK, :] = be
        gate_ref[h * PEER_TOPK:(h + 1) * PEER_TOPK, :] = gate


def _router(h2, wq_t, sk):
    n, d = h2.shape
    tm = TM_DENSE
    col = lambda i: (0, i)
    return pl.pallas_call(
        _router_kernel,
        grid=(n // tm,),
        in_specs=[pl.BlockSpec((tm, d), lambda i: (i, 0)),
                  pl.BlockSpec(wq_t.shape, lambda i: (0, 0)),
                  pl.BlockSpec(sk.shape, lambda i: (0, 0, 0))],
        out_specs=[pl.BlockSpec((PEER_SEL, tm), col), pl.BlockSpec((PEER_SEL, tm), col)],
        out_shape=[jax.ShapeDtypeStruct((PEER_SEL, n), jnp.int32), jax.ShapeDtypeStruct((PEER_SEL, n), F32)],
        scratch_shapes=[pltpu.VMEM((PEER_HEADS * 2, PEER_NKEYS, tm), F32)],
        compiler_params=pltpu.CompilerParams(vmem_limit_bytes=VMEM_LIMIT),
        name="peer_router",
    )(h2, wq_t, sk)


def _gather_kernel(idx_hbm, gate_ref, h2_ref, x1_ref, g2_ref, fg_ref, rep_ref, rept_ref, tab_hbm, out_ref,
                   idx_smem, buf, m_ref, isem, gsem, *, tb, final_norm):
    i = pl.program_id(0)
    nsteps = pl.num_programs(0)
    nchunk = D_MODEL // LANES
    tile_rows = 2 * nchunk
    tok_rows = PEER_SEL * tile_rows

    def idx_copy(block, sl):
        return pltpu.make_async_copy(idx_hbm.at[pl.ds(block * tb, tb), :], idx_smem.at[sl], isem.at[sl])

    def row_copy(sl, t, s, e):
        return pltpu.make_async_copy(tab_hbm.at[e],
                                     buf.at[sl, pl.ds(t * tok_rows + s * tile_rows, tile_rows), :],
                                     gsem.at[sl * tb + t])

    def issue(sl, t, sels):
        for s in sels:
            row_copy(sl, t, s, idx_smem[sl, t, s]).start(priority=s % 2)

    def wait_token(sl, t):
        for s in range(PEER_SEL):
            row_copy(sl, t, s, 0).wait()

    @pl.when(i == 0)
    def _():
        c = idx_copy(0, 0)
        c.start()
        c.wait()

        def tok(t, carry):
            issue(0, t, range(PEER_SEL))
            return carry
        lax.fori_loop(0, tb, tok, 0)
        idx_copy(1, 1).start()

    g2 = g2_ref[0]
    lane16 = lax.broadcasted_iota(jnp.int32, (nchunk, tok_rows), 1) & (tile_rows - 1)
    chunk = lax.broadcasted_iota(jnp.int32, (nchunk, tok_rows), 0)
    down_diag = (lane16 == chunk).astype(F32)
    up_diag = (lane16 == chunk + nchunk).astype(F32)
    sub = lax.broadcasted_iota(jnp.int32, (SUBLANES, LANES), 0)
    zpad = jnp.zeros((nchunk, LANES), F32)
    half = PEER_SEL // 2

    mxu_rows = 256
    nsplit = tok_rows // mxu_rows
    group_copies = SUBLANES * PEER_SEL
    mid_split = group_copies // 16
    per_split = (group_copies - 4 * mid_split) // (2 * SUBLANES * nsplit)

    def tiles(sl, t, k):
        return buf[sl, t * tok_rows + k * mxu_rows:t * tok_rows + (k + 1) * mxu_rows, :]

    def phase(sl):
        for t8 in range(tb // SUBLANES):
            r0 = sl * tb + t8 * SUBLANES
            for j in range(SUBLANES):
                wait_token(sl, t8 * SUBLANES + j)
            todo = [(t8 * SUBLANES + j, s) for j in range(SUBLANES) for s in range(PEER_SEL)]

            def take(n):
                for t, s in todo[:n]:
                    row_copy(1 - sl, t, s, idx_smem[1 - sl, t, s]).start(priority=s % 2)
                del todo[:n]

            for j in range(SUBLANES):
                t = t8 * SUBLANES + j
                x16 = jnp.concatenate([h2_ref[r0 + j], zpad], axis=0).astype(MXU_DTYPE)
                for k in range(nsplit):
                    cols = slice(k * mxu_rows, (k + 1) * mxu_rows)
                    part = _dot_nt(x16, tiles(sl, t, k))[0:nchunk] * down_diag[:, cols]
                    m_ref[j * nchunk:(j + 1) * nchunk, cols] = part
                    take(per_split)
            m_all = m_ref[...]
            m_hi = m_all.astype(MXU_DTYPE)
            m_lo = (m_all - m_hi.astype(F32)).astype(MXU_DTYPE)
            take(mid_split)
            e2 = _dot(jnp.concatenate([m_hi, m_lo], axis=0), rept_ref[...])
            e = e2[0:SUBLANES * nchunk] + e2[SUBLANES * nchunk:]
            take(mid_split)
            d = jnp.zeros((SUBLANES, PEER_SEL), F32)
            for j in range(SUBLANES):
                d = jnp.where(sub == j, jnp.sum(e[j * nchunk:(j + 1) * nchunk], axis=0, keepdims=True), d)
            a = gate_ref[r0:r0 + SUBLANES, :] * jax.nn.gelu(d)
            take(mid_split)
            a_exp = _dot(a.astype(MXU_DTYPE), rep_ref[...])
            take(mid_split)
            for j in range(SUBLANES):
                t = t8 * SUBLANES + j
                v = jnp.broadcast_to(a_exp[j:j + 1], (nchunk, tok_rows)) * up_diag
                v16 = jnp.concatenate([v, jnp.zeros_like(v)], axis=0).astype(MXU_DTYPE)
                y = jnp.zeros((tile_rows, LANES), F32)
                for k in range(nsplit):
                    y = y + _dot(v16[:, k * mxu_rows:(k + 1) * mxu_rows], tiles(sl, t, k))
                    take(per_split)
                x2 = x1_ref[r0 + j] + g2 * y[0:nchunk]
                if final_norm:
                    x2 = x2 * lax.rsqrt(jnp.mean(x2 * x2, keepdims=True) + EPS) * fg_ref[...]
                for c in range(nchunk):
                    out_ref[r0 + j:r0 + j + 1, c * LANES:(c + 1) * LANES] = x2[c:c + 1, :]
            take(len(todo))

    more = i + 1 < nsteps
    idx_copy(2 * i + 1, 1).wait()

    @pl.when(more)
    def _():
        idx_copy(2 * i + 2, 0).start()

    phase(0)

    @pl.when(more)
    def _():
        idx_copy(2 * i + 2, 0).wait()
        idx_copy(2 * i + 3, 1).start()

    phase(1)

    @pl.when(jnp.logical_not(more))
    def _():
        for t in range(tb):
            wait_token(0, t)


def _gather(idx_tm, gate_tm, h2, x1, g2, g2_index, fg, table, final_norm):
    n, nchunk, _ = x1.shape
    tb = TB_GATHER
    step = 2 * tb
    tile_rows = 2 * nchunk
    tok_rows = PEER_SEL * tile_rows
    rep = (jnp.arange(tok_rows)[None, :] // tile_rows == jnp.arange(PEER_SEL)[:, None]).astype(MXU_DTYPE)
    row3 = lambda i: (i, 0, 0)
    const = lambda i: (0, 0)
    return pl.pallas_call(
        functools.partial(_gather_kernel, tb=tb, final_norm=final_norm),
        grid=(n // step,),
        in_specs=[pl.BlockSpec(memory_space=pl.ANY),
                  pl.BlockSpec((step, PEER_SEL), lambda i: (i, 0)),
                  pl.BlockSpec((step, nchunk, LANES), row3), pl.BlockSpec((step, nchunk, LANES), row3),
                  pl.BlockSpec((1, nchunk, LANES), g2_index),
                  pl.BlockSpec((nchunk, LANES), const),
                  pl.BlockSpec((PEER_SEL, tok_rows), const),
                  pl.BlockSpec((tok_rows, PEER_SEL), const),
                  pl.BlockSpec(memory_space=pl.ANY)],
        out_specs=pl.BlockSpec((step, nchunk * LANES), lambda i: (i, 0)),
        out_shape=jax.ShapeDtypeStruct((n, nchunk * LANES), F32),
        scratch_shapes=[pltpu.SMEM((2, tb, PEER_SEL), jnp.int32),
                        pltpu.VMEM((2, tb * tok_rows, LANES), MXU_DTYPE),
                        pltpu.VMEM((SUBLANES * nchunk, tok_rows), F32),
                        pltpu.SemaphoreType.DMA((2,)),
                        pltpu.SemaphoreType.DMA((2 * tb,))],
        compiler_params=pltpu.CompilerParams(dimension_semantics=("arbitrary",), vmem_limit_bytes=VMEM_LIMIT),
        name="peer_gather",
    )(idx_tm, gate_tm, h2, x1, g2, fg, rep, rep.T, table)


def _peer(h2, h2c, x1c, g2, g2_index, wq_t, sk, table, fg, final_norm):
    idx, gate = _router(h2, wq_t, sk)
    return _gather(idx.T, gate.T, h2c, x1c, g2, g2_index, fg, table, final_norm)


def _pack_table(down, up):
    e, d = down.shape
    chunks = (e, d // LANES, LANES)
    return jnp.concatenate([down.reshape(chunks), up.reshape(chunks)], axis=1).astype(MXU_DTYPE)


def _rope_tables(seq):
    quarter = HEAD_DIM // 4
    inv = jnp.power(ROPE_BASE, -jnp.arange(quarter, dtype=F32) / quarter)
    t = jnp.arange(seq)
    row = (t // GRID_W).astype(F32)
    col = (t % GRID_W).astype(F32)
    ang_r = row[:, None] * inv[None, :]
    ang_c = col[:, None] * inv[None, :]
    cos = jnp.concatenate([jnp.cos(ang_r)] * 2 + [jnp.cos(ang_c)] * 2, axis=1)
    sin = jnp.concatenate([-jnp.sin(ang_r), jnp.sin(ang_r), -jnp.sin(ang_c), jnp.sin(ang_c)], axis=1)
    reps = LANES // HEAD_DIM
    return jnp.tile(cos, (1, reps)), jnp.tile(sin, (1, reps))


def _inproj_weights(w_in):
    d = w_in.shape[0]
    slots = []
    for h in range(ATTN_HEADS):
        g = h // GQA_GROUP
        blk = w_in[:, h * HEAD_DIM:(h + 1) * HEAD_DIM]
        z = jnp.zeros((d, HEAD_DIM), w_in.dtype)
        slots.append(jnp.concatenate([blk, z] if g == 0 else [z, blk], axis=1))
    return jnp.concatenate(slots + [w_in[:, Q_W:]], axis=1).astype(MXU_DTYPE)


def _outproj_weights(w_out):
    d = w_out.shape[1]
    slots = []
    for h in range(ATTN_HEADS):
        g = h // GQA_GROUP
        blk = w_out[h * HEAD_DIM:(h + 1) * HEAD_DIM]
        z = jnp.zeros((HEAD_DIM, d), w_out.dtype)
        slots.append(jnp.concatenate([blk, z] if g == 0 else [z, blk], axis=0))
    return jnp.concatenate(slots, axis=0).astype(MXU_DTYPE), w_out[Q_W:].astype(MXU_DTYPE)


def kernel(x, c, ctx, c_ctx, mod_w, mod_b, norm1_g, norm2_g, attn_in_w, attn_out_w, attn_sink, gmlp_w_s, gmlp_b_s, pool_w, pool_scale, peer_w_q, peer_sub_keys, peer_down, peer_up, final_g):
    batch, seq, d = x.shape
    n = batch * seq
    n_ctx = ctx.shape[1]
    depth = mod_w.shape[0]
    assert depth == 2 and d == D_MODEL
    assert seq % TM_DENSE == 0 and n_ctx % TM_DENSE == 0 and LANES % (2 * TB_GATHER) == 0

    mod_rows = -(-(batch + 1) // SUBLANES) * SUBLANES
    cs = jnp.zeros((mod_rows, d), F32).at[:batch].set(c).at[batch].set(c_ctx)
    mods = _modulation(cs, mod_w, mod_b)
    seq_tiles = seq // TM_DENSE
    main_mod = lambda i: (i // seq_tiles, 0, 0)
    ctx_mod = lambda i: (batch, 0, 0)
    gather_mod = lambda i: (i // (seq // (2 * TB_GATHER)), 0, 0)

    x2 = x.reshape(n, d)
    row1 = lambda v: v.reshape(1, -1)

    w_ext = _inproj_weights(attn_in_w[0])
    cos, sin = _rope_tables(seq)
    m64 = jnp.kron(jnp.eye(GMLP_GROUPS, dtype=F32), jnp.full((GMLP_DIM, GMLP_DIM), 1.0 / GMLP_DIM, F32)
                   ).astype(MXU_DTYPE)
    q, k, v, u, gn = _inproj(x2, mods[0], main_mod, row1(norm1_g[0]), w_ext, cos, sin, m64, seq_tiles)
    ones = jnp.ones((n_ctx, LANES), F32)
    _, kx, vx, _, _ = _inproj(ctx.reshape(batch * n_ctx, d), mods[0], ctx_mod, row1(norm1_g[0]), w_ext,
                              ones, jnp.zeros_like(ones), m64, n_ctx // TM_DENSE)
    ws = gmlp_w_s[0].astype(MXU_DTYPE)
    bt = gmlp_b_s[0]
    bs = jnp.stack([jnp.concatenate([jnp.broadcast_to(bt[2 * j][:, None], (CHUNK, GMLP_DIM)),
                                     jnp.broadcast_to(bt[2 * j + 1][:, None], (CHUNK, GMLP_DIM))], axis=1)
                    for j in range(GMLP_GROUPS // 2)])
    attn, mix = _attn_gmlp(attn_sink[0], q, k, v, kx, vx, u, gn, ws, bs, batch, seq)
    wa, wm = _outproj_weights(attn_out_w[0])
    x1c, h2, h2c = _outproj(x2, attn, mix, wa, wm, mods[0], main_mod, row1(norm2_g[0]))
    chunked = lambda v: v.reshape(v.shape[:-1] + (d // LANES, LANES))
    fg = chunked(final_g)
    xl = _peer(h2, h2c, x1c, chunked(mods[0][:, 5]), gather_mod, peer_w_q[0].T.astype(MXU_DTYPE),
               peer_sub_keys[0].astype(MXU_DTYPE), _pack_table(peer_down[0], peer_up[0]), fg, False)

    x1c, h2, h2c = _pool(xl, mods[1], main_mod, row1(norm1_g[1]), row1(norm2_g[1]), pool_w[0].astype(MXU_DTYPE),
                         row1(pool_scale[0]), seq)
    out = _peer(h2, h2c, x1c, chunked(mods[1][:, 5]), gather_mod, peer_w_q[1].T.astype(MXU_DTYPE),
                peer_sub_keys[1].astype(MXU_DTYPE), _pack_table(peer_down[1], peer_up[1]), fg, True)
    return out.reshape(batch, seq, d)
```

```python
import functools

import jax
import jax.numpy as jnp
from jax import lax
from jax.experimental import pallas as pl
from jax.experimental.pallas import tpu as pltpu

D_MODEL = 1024
GRID_W = 64
HEAD_DIM = 64
ATTN_HEADS = 8
KV_HEADS = 2
GQA_GROUP = ATTN_HEADS // KV_HEADS
WINDOW = 128
ATTN_BLOCK = 128
ROPE_BASE = 10000.0
GMLP_GROUPS = 8
GMLP_DIM = 64
CHUNK = 128
Q_W = ATTN_HEADS * HEAD_DIM
KV_W = KV_HEADS * HEAD_DIM
A_W = GMLP_GROUPS * GMLP_DIM
POOL_WINDOWS = (2, 4, 8, 16)
POOL_GROUPS = 4
POOL_DIM = D_MODEL // POOL_GROUPS
POOL_HALO = 8
PEER_HEADS = 8
PEER_NKEYS = 128
PEER_TOPK = 16
PEER_DKEY = 256
PEER_DHALF = PEER_DKEY // 2
PEER_SEL = PEER_HEADS * PEER_TOPK
N_MOD = 6
EPS = 1e-6

LANES = 128
SUBLANES = 8
QPAD_W = ATTN_HEADS * LANES
MXU_DTYPE = jnp.bfloat16
NEG = -1e30
F32 = jnp.float32

TM_DENSE = 256
TB_GATHER = 16
VMEM_LIMIT = 48 * 1024 * 1024


def _rms(x):
    return x * lax.rsqrt(jnp.mean(x * x, axis=-1, keepdims=True) + EPS)


def _dot(a, b):
    return jnp.dot(a, b, preferred_element_type=F32)


def _dot_nt(a, b):
    return lax.dot_general(a, b, (((1,), (1,)), ((), ())), preferred_element_type=F32)


def _mod_kernel(c_ref, w_ref, b_ref, o_ref):
    c = c_ref[...]
    s = c * jax.nn.sigmoid(c)
    o_ref[0] = jnp.dot(s, w_ref[0], preferred_element_type=F32, precision=lax.Precision.HIGHEST) + b_ref[0]


def _modulation(cs, mod_w, mod_b):
    depth, d, n6 = mod_w.shape
    r = cs.shape[0]
    tn = 512
    out = pl.pallas_call(
        _mod_kernel,
        grid=(depth, n6 // tn),
        in_specs=[pl.BlockSpec((r, d), lambda l, j: (0, 0)),
                  pl.BlockSpec((1, d, tn), lambda l, j: (l, 0, j)),
                  pl.BlockSpec((1, 1, tn), lambda l, j: (l, 0, j))],
        out_specs=pl.BlockSpec((1, r, tn), lambda l, j: (l, 0, j)),
        out_shape=jax.ShapeDtypeStruct((depth, r, n6), F32),
        name="adaln_modulation",
    )(cs, mod_w, mod_b.reshape(depth, 1, n6))
    return out.reshape(depth, r, N_MOD, d)


def _rope(t, cos, sin, first_half):
    partner = jnp.where(first_half, pltpu.roll(t, LANES - 16, 1), pltpu.roll(t, 16, 1))
    return t * cos + partner * sin


def _inproj_kernel(x_ref, mod_ref, g_ref, w_ref, cos_ref, sin_ref, m64_ref,
                   q_ref, k_ref, v_ref, u_ref, vn_ref):
    x = x_ref[...]
    sh = mod_ref[0, 0:1, :]
    sc = mod_ref[0, 1:2, :]
    h = _rms(x) * g_ref[...] * (1.0 + sc) + sh
    p = _dot(h.astype(MXU_DTYPE), w_ref[...])
    cos = cos_ref[...]
    sin = sin_ref[...]
    lane = lax.broadcasted_iota(jnp.int32, cos.shape, 1)
    first_half = (lane & 31) < 16
    scale = HEAD_DIM ** -0.5
    for s in range(ATTN_HEADS):
        t = _rope(p[:, s * LANES:(s + 1) * LANES], cos, sin, first_half)
        q_ref[:, s * LANES:(s + 1) * LANES] = (t * scale).astype(q_ref.dtype)
    o = QPAD_W
    k_ref[...] = _rope(p[:, o:o + KV_W], cos, sin, first_half).astype(k_ref.dtype)
    v_ref[...] = p[:, o + KV_W:o + 2 * KV_W].astype(v_ref.dtype)
    o += 2 * KV_W
    u_ref[...] = jax.nn.gelu(p[:, o:o + A_W]).astype(u_ref.dtype)
    vg = jax.nn.gelu(p[:, o + A_W:o + 2 * A_W])
    m64 = m64_ref[...]
    mu = _dot(vg.astype(MXU_DTYPE), m64)
    cen = vg - mu
    var = _dot((cen * cen).astype(MXU_DTYPE), m64)
    vn_ref[...] = (cen * lax.rsqrt(var + EPS)).astype(vn_ref.dtype)


def _inproj(x2, mod, mod_index, g, w_ext, cos, sin, m64, seq_tiles):
    n, d = x2.shape
    tm = TM_DENSE
    wcols = w_ext.shape[1]
    row = lambda i: (i, 0)
    const = lambda i: (0, 0)
    outs = [jax.ShapeDtypeStruct((n, QPAD_W), MXU_DTYPE),
            jax.ShapeDtypeStruct((n, KV_W), MXU_DTYPE),
            jax.ShapeDtypeStruct((n, KV_W), MXU_DTYPE),
            jax.ShapeDtypeStruct((n, A_W), MXU_DTYPE),
            jax.ShapeDtypeStruct((n, A_W), MXU_DTYPE)]
    return pl.pallas_call(
        _inproj_kernel,
        grid=(n // tm,),
        in_specs=[pl.BlockSpec((tm, d), row),
                  pl.BlockSpec((1, N_MOD, d), mod_index),
                  pl.BlockSpec((1, d), const),
                  pl.BlockSpec((d, wcols), const),
                  pl.BlockSpec((tm, LANES), lambda i: (i % seq_tiles, 0)),
                  pl.BlockSpec((tm, LANES), lambda i: (i % seq_tiles, 0)),
                  pl.BlockSpec((A_W, A_W), const)],
        out_specs=[pl.BlockSpec((tm, QPAD_W), row), pl.BlockSpec((tm, KV_W), row), pl.BlockSpec((tm, KV_W), row),
                   pl.BlockSpec((tm, A_W), row), pl.BlockSpec((tm, A_W), row)],
        out_shape=outs,
        compiler_params=pltpu.CompilerParams(vmem_limit_bytes=VMEM_LIMIT),
        name="inproj_rope",
    )(x2, mod, g, w_ext, cos, sin, m64)


def _attn_gmlp_kernel(sink_ref, q_ref, kp_ref, kc_ref, kn_ref, vp_ref, vc_ref, vn_ref, kx_ref, vx_ref,
                      u_ref, g_ref, ws_ref, bs_ref, attn_ref, mix_ref):
    n = pl.program_id(1)
    nb = pl.num_programs(1)
    rows = GQA_GROUP * ATTN_BLOCK
    ri = lax.broadcasted_iota(jnp.int32, (rows, ATTN_BLOCK), 0) & (ATTN_BLOCK - 1)
    ci = lax.broadcasted_iota(jnp.int32, (rows, ATTN_BLOCK), 1)
    mask_prev = jnp.logical_and(ci >= ri, n > 0)
    mask_next = jnp.logical_and(ci <= ri, n < nb - 1)
    rblk = lax.broadcasted_iota(jnp.int32, (rows, 1), 0) >> (ATTN_BLOCK.bit_length() - 1)
    kp, kc, kn, kx = kp_ref[...], kc_ref[...], kn_ref[...], kx_ref[...]
    vp, vc, vn, vx = vp_ref[...], vc_ref[...], vn_ref[...], vx_ref[...]
    for g in range(KV_HEADS):
        h0 = g * GQA_GROUP
        qg = jnp.concatenate([q_ref[:, (h0 + j) * LANES:(h0 + j + 1) * LANES] for j in range(GQA_GROUP)], axis=0)
        sink = jnp.full((rows, 1), sink_ref[h0], F32)
        for j in range(1, GQA_GROUP):
            sink = jnp.where(rblk == j, sink_ref[h0 + j], sink)
        s_p = jnp.where(mask_prev, _dot_nt(qg, kp), NEG)
        s_c = _dot_nt(qg, kc)
        s_n = jnp.where(mask_next, _dot_nt(qg, kn), NEG)
        s_x = _dot_nt(qg, kx)
        m = jnp.maximum(jnp.maximum(jnp.max(s_p, axis=1, keepdims=True), jnp.max(s_c, axis=1, keepdims=True)),
                        jnp.maximum(jnp.max(s_n, axis=1, keepdims=True), jnp.max(s_x, axis=1, keepdims=True)))
        m = jnp.maximum(m, sink)
        p_p, p_c, p_n, p_x = jnp.exp(s_p - m), jnp.exp(s_c - m), jnp.exp(s_n - m), jnp.exp(s_x - m)
        den = (jnp.sum(p_p, axis=1, keepdims=True) + jnp.sum(p_c, axis=1, keepdims=True)
               + jnp.sum(p_n, axis=1, keepdims=True) + jnp.sum(p_x, axis=1, keepdims=True) + jnp.exp(sink - m))
        o = (_dot(p_p.astype(MXU_DTYPE), vp) + _dot(p_c.astype(MXU_DTYPE), vc)
             + _dot(p_n.astype(MXU_DTYPE), vn) + _dot(p_x.astype(MXU_DTYPE), vx))
        o = o / den
        for j in range(GQA_GROUP):
            attn_ref[:, (h0 + j) * LANES:(h0 + j + 1) * LANES] = (
                o[j * ATTN_BLOCK:(j + 1) * ATTN_BLOCK].astype(attn_ref.dtype))
    lane = lax.broadcasted_iota(jnp.int32, (CHUNK, LANES), 1)
    for j in range(GMLP_GROUPS // 2):
        gp = g_ref[:, j * LANES:(j + 1) * LANES]
        mixed = jnp.where(lane < GMLP_DIM, _dot(ws_ref[2 * j], gp), _dot(ws_ref[2 * j + 1], gp)) + bs_ref[j]
        mix_ref[:, j * LANES:(j + 1) * LANES] = (u_ref[:, j * LANES:(j + 1) * LANES].astype(F32) * mixed
                                                 ).astype(mix_ref.dtype)


def _attn_gmlp(sink, q, k, v, kx, vx, u, gn, ws, bs, batch, seq):
    n = q.shape[0]
    nb = seq // ATTN_BLOCK
    cx = kx.shape[0] // batch
    blk = ATTN_BLOCK
    cur = lambda b, i: (b * nb + i, 0)
    prev = lambda b, i: (b * nb + jnp.maximum(i - 1, 0), 0)
    nxt = lambda b, i: (b * nb + jnp.minimum(i + 1, nb - 1), 0)
    ctx = lambda b, i: (b, 0)
    return pl.pallas_call(
        _attn_gmlp_kernel,
        grid=(batch, nb),
        in_specs=[pl.BlockSpec(memory_space=pltpu.SMEM),
                  pl.BlockSpec((blk, QPAD_W), cur),
                  pl.BlockSpec((blk, KV_W), prev), pl.BlockSpec((blk, KV_W), cur), pl.BlockSpec((blk, KV_W), nxt),
                  pl.BlockSpec((blk, KV_W), prev), pl.BlockSpec((blk, KV_W), cur), pl.BlockSpec((blk, KV_W), nxt),
                  pl.BlockSpec((cx, KV_W), ctx), pl.BlockSpec((cx, KV_W), ctx),
                  pl.BlockSpec((blk, A_W), cur), pl.BlockSpec((blk, A_W), cur),
                  pl.BlockSpec(ws.shape, lambda b, i: (0, 0, 0)),
                  pl.BlockSpec(bs.shape, lambda b, i: (0, 0, 0))],
        out_specs=[pl.BlockSpec((blk, QPAD_W), cur), pl.BlockSpec((blk, A_W), cur)],
        out_shape=[jax.ShapeDtypeStruct((n, QPAD_W), MXU_DTYPE), jax.ShapeDtypeStruct((n, A_W), MXU_DTYPE)],
        compiler_params=pltpu.CompilerParams(vmem_limit_bytes=VMEM_LIMIT),
        name="window_attn_gmlp",
    )(sink, q, k, k, k, v, v, v, kx, vx, u, gn, ws, bs)


def _residual_norm2(x, y, mod_ref, g2_ref, x1c_ref, h2_ref, h2c_ref):
    g1 = mod_ref[0, 2:3, :]
    sh2 = mod_ref[0, 3:4, :]
    sc2 = mod_ref[0, 4:5, :]
    x1 = x + g1 * y
    h2 = _rms(x1) * g2_ref[...] * (1.0 + sc2) + sh2
    h2_ref[...] = h2
    for c in range(x1.shape[1] // LANES):
        x1c_ref[:, c, :] = x1[:, c * LANES:(c + 1) * LANES]
        h2c_ref[:, c, :] = h2[:, c * LANES:(c + 1) * LANES]


def _outproj_kernel(x_ref, a_ref, m_ref, wa_ref, wm_ref, mod_ref, g2_ref, x1c_ref, h2_ref, h2c_ref):
    y = _dot(a_ref[...], wa_ref[...]) + _dot(m_ref[...], wm_ref[...])
    _residual_norm2(x_ref[...], y, mod_ref, g2_ref, x1c_ref, h2_ref, h2c_ref)


def _mixer_outputs(n, d, tm):
    chunk3 = (n, d // LANES, LANES)
    specs = [pl.BlockSpec((tm, d // LANES, LANES), lambda i: (i, 0, 0)),
             pl.BlockSpec((tm, d), lambda i: (i, 0)),
             pl.BlockSpec((tm, d // LANES, LANES), lambda i: (i, 0, 0))]
    shapes = [jax.ShapeDtypeStruct(chunk3, F32), jax.ShapeDtypeStruct((n, d), F32), jax.ShapeDtypeStruct(chunk3, F32)]
    return specs, shapes


def _outproj(x2, attn, mix, wa, wm, mod, mod_index, g2):
    n, d = x2.shape
    tm = TM_DENSE
    row = lambda i: (i, 0)
    const = lambda i: (0, 0)
    out_specs, out_shape = _mixer_outputs(n, d, tm)
    return pl.pallas_call(
        _outproj_kernel,
        grid=(n // tm,),
        in_specs=[pl.BlockSpec((tm, d), row), pl.BlockSpec((tm, QPAD_W), row), pl.BlockSpec((tm, A_W), row),
                  pl.BlockSpec(wa.shape, const), pl.BlockSpec(wm.shape, const),
                  pl.BlockSpec((1, N_MOD, d), mod_index), pl.BlockSpec((1, d), const)],
        out_specs=out_specs,
        out_shape=out_shape,
        compiler_params=pltpu.CompilerParams(vmem_limit_bytes=VMEM_LIMIT),
        name="outproj_residual",
    )(x2, attn, mix, wa, wm, mod, g2)


def _pool_kernel(xp_ref, xc_ref, xn_ref, mod_ref, g1_ref, g2_ref, wp_ref, ps_ref, x1c_ref, h2_ref, h2c_ref, hs_ref,
                 *, seq_tiles, seq):
    i = pl.program_id(0)
    tm = xc_ref.shape[0]
    sh = mod_ref[0, 0:1, :]
    sc = mod_ref[0, 1:2, :]
    g1 = g1_ref[...]

    def norm(x):
        return _rms(x) * g1 * (1.0 + sc) + sh

    it = i % seq_tiles
    x = xc_ref[...]
    hs_ref[0:POOL_HALO, :] = jnp.where(it > 0, norm(xp_ref[...]), 0.0)
    hs_ref[POOL_HALO:POOL_HALO + tm, :] = norm(x)
    hs_ref[POOL_HALO + tm:, :] = jnp.where(it < seq_tiles - 1, norm(xn_ref[...]), 0.0)
    pos = it * tm + lax.broadcasted_iota(jnp.int32, (tm, 1), 0)
    y_parts = []
    for g, w in enumerate(POOL_WINDOWS):
        cols = slice(g * POOL_DIM, (g + 1) * POOL_DIM)
        tot = hs_ref[POOL_HALO - w // 2:POOL_HALO - w // 2 + tm, cols]
        for dlt in range(-w // 2 + 1, w // 2):
            tot = tot + hs_ref[POOL_HALO + dlt:POOL_HALO + dlt + tm, cols]
        cnt = (jnp.minimum(pos + w // 2, seq) - jnp.maximum(pos - w // 2, 0)).astype(F32)
        pooled = tot / cnt - hs_ref[POOL_HALO:POOL_HALO + tm, cols]
        y_parts.append(_dot(pooled.astype(MXU_DTYPE), wp_ref[g]))
    y = jnp.concatenate(y_parts, axis=1) * ps_ref[...]
    _residual_norm2(x, y, mod_ref, g2_ref, x1c_ref, h2_ref, h2c_ref)


def _pool(x2, mod, mod_index, g1, g2, wp, ps, seq):
    n, d = x2.shape
    tm = TM_DENSE
    seq_tiles = seq // tm
    hb = tm // POOL_HALO
    nh = n // POOL_HALO
    row = lambda i: (i, 0)
    const = lambda i: (0, 0)
    return pl.pallas_call(
        functools.partial(_pool_kernel, seq_tiles=seq_tiles, seq=seq),
        grid=(n // tm,),
        in_specs=[pl.BlockSpec((POOL_HALO, d), lambda i: (jnp.maximum(i * hb - 1, 0), 0)),
                  pl.BlockSpec((tm, d), row),
                  pl.BlockSpec((POOL_HALO, d), lambda i: (jnp.minimum((i + 1) * hb, nh - 1), 0)),
                  pl.BlockSpec((1, N_MOD, d), mod_index),
                  pl.BlockSpec((1, d), const), pl.BlockSpec((1, d), const),
                  pl.BlockSpec(wp.shape, lambda i: (0, 0, 0)), pl.BlockSpec((1, d), const)],
        out_specs=_mixer_outputs(n, d, tm)[0],
        out_shape=_mixer_outputs(n, d, tm)[1],
        scratch_shapes=[pltpu.VMEM((tm + 2 * POOL_HALO, d), F32)],
        compiler_params=pltpu.CompilerParams(vmem_limit_bytes=VMEM_LIMIT),
        name="pool_mixer_residual",
    )(x2, x2, x2, mod, g1, g2, wp, ps)


def _top16(s):
    t = s.shape[1]
    rows = lax.broadcasted_iota(jnp.int32, s.shape, 0).astype(F32)
    r16 = lax.broadcasted_iota(jnp.int32, (PEER_TOPK, t), 0)

    def body(r, carry):
        s, tv, ti = carry
        m = jnp.max(s, axis=0, keepdims=True)
        am = jnp.min(jnp.where(s == m, rows, float(PEER_NKEYS)), axis=0, keepdims=True)
        tv = jnp.where(r16 == r, m, tv)
        ti = jnp.where(r16 == r, am, ti)
        s = jnp.where(rows == am, -jnp.inf, s)
        return s, tv, ti

    _, tv, ti = lax.fori_loop(0, PEER_TOPK, body,
                              (s, jnp.zeros((PEER_TOPK, t), F32), jnp.zeros((PEER_TOPK, t), F32)))
    return tv, ti


def _combine16(tv1, ti1, tv2, ti2):
    k = PEER_TOPK
    t = tv1.shape[1]
    half = k // 2
    sv = [tv1[0:1] + tv2]
    ev = [ti1[0:1] * PEER_NKEYS + ti2]
    for a in range(1, half):
        sv.append(tv1[a:a + 1] + tv2[0:half])
        ev.append(ti1[a:a + 1] * PEER_NKEYS + ti2[0:half])
    sv.append(tv1[half:k] + tv2[0:1])
    ev.append(ti1[half:k] * PEER_NKEYS + ti2[0:1])
    cand = jnp.concatenate(sv, axis=0)
    cexp = jnp.concatenate(ev, axis=0)
    nrow = cand.shape[0]
    r = lax.broadcasted_iota(jnp.int32, (nrow, t), 0)
    mid = r - k
    hbits = half.bit_length() - 1
    flat = jnp.where(r < k, r,
                     jnp.where(r < k + (half - 1) * half, ((mid >> hbits) + 1) * k + (mid & (half - 1)),
                               (r - (k + (half - 1) * half) + half) * k)).astype(F32)
    r16 = lax.broadcasted_iota(jnp.int32, (k, t), 0)

    def body(j, carry):
        cand, bs, be = carry
        m = jnp.max(cand, axis=0, keepdims=True)
        sel = jnp.min(jnp.where(cand == m, flat, float(k * k)), axis=0, keepdims=True)
        hit = flat == sel
        e = jnp.max(jnp.where(hit, cexp, -1.0), axis=0, keepdims=True)
        bs = jnp.where(r16 == j, m, bs)
        be = jnp.where(r16 == j, e, be)
        cand = jnp.where(hit, -jnp.inf, cand)
        return cand, bs, be

    _, bs, be = lax.fori_loop(0, k, body, (cand, jnp.zeros((k, t), F32), jnp.zeros((k, t), F32)))
    w = jnp.exp(bs - bs[0:1])
    gate = w / jnp.sum(w, axis=0, keepdims=True)
    return gate, be.astype(jnp.int32)


def _router_kernel(h2_ref, wq_ref, sk_ref, idx_ref, gate_ref, st_ref):
    hb = h2_ref[...].astype(MXU_DTYPE)
    qt = _dot_nt(wq_ref[...], hb)
    for hp in range(PEER_HEADS * 2):
        blk = qt[hp * PEER_DHALF:(hp + 1) * PEER_DHALF].astype(MXU_DTYPE)
        st_ref[hp] = _dot(sk_ref[hp % 2], blk)
    for h in range(PEER_HEADS):
        tv1, ti1 = _top16(st_ref[2 * h])
        tv2, ti2 = _top16(st_ref[2 * h + 1])
        gate, be = _combine16(tv1, ti1, tv2, ti2)
        idx_ref[h * PEER_TOPK:(h + 1) * PEER_TOPK, :] = be
        gate_ref[h * PEER_TOPK:(h + 1) * PEER_TOPK, :] = gate


def _router(h2, wq_t, sk):
    n, d = h2.shape
    tm = TM_DENSE
    col = lambda i: (0, i)
    return pl.pallas_call(
        _router_kernel,
        grid=(n // tm,),
        in_specs=[pl.BlockSpec((tm, d), lambda i: (i, 0)),
                  pl.BlockSpec(wq_t.shape, lambda i: (0, 0)),
                  pl.BlockSpec(sk.shape, lambda i: (0, 0, 0))],
        out_specs=[pl.BlockSpec((PEER_SEL, tm), col), pl.BlockSpec((PEER_SEL, tm), col)],
        out_shape=[jax.ShapeDtypeStruct((PEER_SEL, n), jnp.int32), jax.ShapeDtypeStruct((PEER_SEL, n), F32)],
        scratch_shapes=[pltpu.VMEM((PEER_HEADS * 2, PEER_NKEYS, tm), F32)],
        compiler_params=pltpu.CompilerParams(vmem_limit_bytes=VMEM_LIMIT),
        name="peer_router",
    )(h2, wq_t, sk)


def _gather_kernel(idx_hbm, gate_ref, h2_ref, x1_ref, g2_ref, rep_ref, rept_ref, tab_hbm, out_ref,
                   idx_smem, buf, m_ref, isem, gsem, *, tb):
    i = pl.program_id(0)
    nsteps = pl.num_programs(0)
    nchunk = D_MODEL // LANES
    tile_rows = 2 * nchunk
    tok_rows = PEER_SEL * tile_rows

    def idx_copy(block, sl):
        return pltpu.make_async_copy(idx_hbm.at[pl.ds(block * tb, tb), :], idx_smem.at[sl], isem.at[sl])

    def row_copy(sl, t, s, e):
        return pltpu.make_async_copy(tab_hbm.at[e],
                                     buf.at[sl, pl.ds(t * tok_rows + s * tile_rows, tile_rows), :],
                                     gsem.at[sl * tb + t])

    def issue(sl, t, sels):
        for s in sels:
            row_copy(sl, t, s, idx_smem[sl, t, s]).start(priority=s % 2)

    def wait_token(sl, t):
        for s in range(PEER_SEL):
            row_copy(sl, t, s, 0).wait()

    @pl.when(i == 0)
    def _():
        c = idx_copy(0, 0)
        c.start()
        c.wait()

        def tok(t, carry):
            issue(0, t, range(PEER_SEL))
            return carry
        lax.fori_loop(0, tb, tok, 0)
        idx_copy(1, 1).start()

    g2 = g2_ref[0]
    lane16 = lax.broadcasted_iota(jnp.int32, (nchunk, tok_rows), 1) & (tile_rows - 1)
    chunk = lax.broadcasted_iota(jnp.int32, (nchunk, tok_rows), 0)
    down_diag = (lane16 == chunk).astype(F32)
    up_diag = (lane16 == chunk + nchunk).astype(F32)
    sub = lax.broadcasted_iota(jnp.int32, (SUBLANES, LANES), 0)
    zpad = jnp.zeros((nchunk, LANES), F32)
    half = PEER_SEL // 2

    mxu_rows = 256
    nsplit = tok_rows // mxu_rows
    group_copies = SUBLANES * PEER_SEL
    mid_split = group_copies // 16
    per_split = (group_copies - 4 * mid_split) // (2 * SUBLANES * nsplit)

    def tiles(sl, t, k):
        return buf[sl, t * tok_rows + k * mxu_rows:t * tok_rows + (k + 1) * mxu_rows, :]

    def phase(sl):
        for t8 in range(tb // SUBLANES):
            r0 = sl * tb + t8 * SUBLANES
            for j in range(SUBLANES):
                wait_token(sl, t8 * SUBLANES + j)
            todo = [(t8 * SUBLANES + j, s) for j in range(SUBLANES) for s in range(PEER_SEL)]

            def take(n):
                for t, s in todo[:n]:
                    row_copy(1 - sl, t, s, idx_smem[1 - sl, t, s]).start(priority=s % 2)
                del todo[:n]

            for j in range(SUBLANES):
                t = t8 * SUBLANES + j
                x16 = jnp.concatenate([h2_ref[r0 + j], zpad], axis=0).astype(MXU_DTYPE)
                for k in range(nsplit):
                    cols = slice(k * mxu_rows, (k + 1) * mxu_rows)
                    part = _dot_nt(x16, tiles(sl, t, k))[0:nchunk] * down_diag[:, cols]
                    m_ref[j * nchunk:(j + 1) * nchunk, cols] = part
                    take(per_split)
            m_all = m_ref[...]
            m_hi = m_all.astype(MXU_DTYPE)
            m_lo = (m_all - m_hi.astype(F32)).astype(MXU_DTYPE)
            take(mid_split)
            e2 = _dot(jnp.concatenate([m_hi, m_lo], axis=0), rept_ref[...])
            e = e2[0:SUBLANES * nchunk] + e2[SUBLANES * nchunk:]
            take(mid_split)
            d = jnp.zeros((SUBLANES, PEER_SEL), F32)
            for j in range(SUBLANES):
                d = jnp.where(sub == j, jnp.sum(e[j * nchunk:(j + 1) * nchunk], axis=0, keepdims=True), d)
            a = gate_ref[r0:r0 + SUBLANES, :] * jax.nn.gelu(d)
            take(mid_split)
            a_exp = _dot(a.astype(MXU_DTYPE), rep_ref[...])
            take(mid_split)
            for j in range(SUBLANES):
                t = t8 * SUBLANES + j
                v = jnp.broadcast_to(a_exp[j:j + 1], (nchunk, tok_rows)) * up_diag
                v16 = jnp.concatenate([v, jnp.zeros_like(v)], axis=0).astype(MXU_DTYPE)
                y = jnp.zeros((tile_rows, LANES), F32)
                for k in range(nsplit):
                    y = y + _dot(v16[:, k * mxu_rows:(k + 1) * mxu_rows], tiles(sl, t, k))
                    take(per_split)
                x2 = x1_ref[r0 + j] + g2 * y[0:nchunk]
                for c in range(nchunk):
                    out_ref[r0 + j:r0 + j + 1, c * LANES:(c + 1) * LANES] = x2[c:c + 1, :]
            take(len(todo))

    more = i + 1 < nsteps
    idx_copy(2 * i + 1, 1).wait()

    @pl.when(more)
    def _():
        idx_copy(2 * i + 2, 0).start()

    phase(0)

    @pl.when(more)
    def _():
        idx_copy(2 * i + 2, 0).wait()
        idx_copy(2 * i + 3, 1).start()

    phase(1)

    @pl.when(jnp.logical_not(more))
    def _():
        for t in range(tb):
            wait_token(0, t)


def _gather(idx_tm, gate_tm, h2, x1, g2, g2_index, table):
    n, nchunk, _ = x1.shape
    tb = TB_GATHER
    step = 2 * tb
    tile_rows = 2 * nchunk
    tok_rows = PEER_SEL * tile_rows
    rep = (jnp.arange(tok_rows)[None, :] // tile_rows == jnp.arange(PEER_SEL)[:, None]).astype(MXU_DTYPE)
    row3 = lambda i: (i, 0, 0)
    const = lambda i: (0, 0)
    return pl.pallas_call(
        functools.partial(_gather_kernel, tb=tb),
        grid=(n // step,),
        in_specs=[pl.BlockSpec(memory_space=pl.ANY),
                  pl.BlockSpec((step, PEER_SEL), lambda i: (i, 0)),
                  pl.BlockSpec((step, nchunk, LANES), row3), pl.BlockSpec((step, nchunk, LANES), row3),
                  pl.BlockSpec((1, nchunk, LANES), g2_index),
                  pl.BlockSpec((PEER_SEL, tok_rows), const),
                  pl.BlockSpec((tok_rows, PEER_SEL), const),
                  pl.BlockSpec(memory_space=pl.ANY)],
        out_specs=pl.BlockSpec((step, nchunk * LANES), lambda i: (i, 0)),
        out_shape=jax.ShapeDtypeStruct((n, nchunk * LANES), F32),
        scratch_shapes=[pltpu.SMEM((2, tb, PEER_SEL), jnp.int32),
                        pltpu.VMEM((2, tb * tok_rows, LANES), MXU_DTYPE),
                        pltpu.VMEM((SUBLANES * nchunk, tok_rows), F32),
                        pltpu.SemaphoreType.DMA((2,)),
                        pltpu.SemaphoreType.DMA((2 * tb,))],
        compiler_params=pltpu.CompilerParams(dimension_semantics=("arbitrary",), vmem_limit_bytes=VMEM_LIMIT),
        name="peer_gather",
    )(idx_tm, gate_tm, h2, x1, g2, rep, rep.T, table)


def _peer(h2, h2c, x1c, g2, g2_index, wq_t, sk, table):
    idx, gate = _router(h2, wq_t, sk)
    return _gather(idx.T, gate.T, h2c, x1c, g2, g2_index, table)


def _final_norm_kernel(x_ref, g_ref, o_ref):
    o_ref[...] = _rms(x_ref[...]) * g_ref[...]


def _final_norm(x2, g):
    n, d = x2.shape
    tm = 2 * TM_DENSE
    return pl.pallas_call(
        _final_norm_kernel,
        grid=(n // tm,),
        in_specs=[pl.BlockSpec((tm, d), lambda i: (i, 0)), pl.BlockSpec((1, d), lambda i: (0, 0))],
        out_specs=pl.BlockSpec((tm, d), lambda i: (i, 0)),
        out_shape=jax.ShapeDtypeStruct((n, d), F32),
        name="final_rmsnorm",
    )(x2, g)


def _pack_table(down, up):
    e, d = down.shape
    chunks = (e, d // LANES, LANES)
    return jnp.concatenate([down.reshape(chunks), up.reshape(chunks)], axis=1).astype(MXU_DTYPE)


def _rope_tables(seq):
    quarter = HEAD_DIM // 4
    inv = jnp.power(ROPE_BASE, -jnp.arange(quarter, dtype=F32) / quarter)
    t = jnp.arange(seq)
    row = (t // GRID_W).astype(F32)
    col = (t % GRID_W).astype(F32)
    ang_r = row[:, None] * inv[None, :]
    ang_c = col[:, None] * inv[None, :]
    cos = jnp.concatenate([jnp.cos(ang_r)] * 2 + [jnp.cos(ang_c)] * 2, axis=1)
    sin = jnp.concatenate([-jnp.sin(ang_r), jnp.sin(ang_r), -jnp.sin(ang_c), jnp.sin(ang_c)], axis=1)
    reps = LANES // HEAD_DIM
    return jnp.tile(cos, (1, reps)), jnp.tile(sin, (1, reps))


def _inproj_weights(w_in):
    d = w_in.shape[0]
    slots = []
    for h in range(ATTN_HEADS):
        g = h // GQA_GROUP
        blk = w_in[:, h * HEAD_DIM:(h + 1) * HEAD_DIM]
        z = jnp.zeros((d, HEAD_DIM), w_in.dtype)
        slots.append(jnp.concatenate([blk, z] if g == 0 else [z, blk], axis=1))
    return jnp.concatenate(slots + [w_in[:, Q_W:]], axis=1).astype(MXU_DTYPE)


def _outproj_weights(w_out):
    d = w_out.shape[1]
    slots = []
    for h in range(ATTN_HEADS):
        g = h // GQA_GROUP
        blk = w_out[h * HEAD_DIM:(h + 1) * HEAD_DIM]
        z = jnp.zeros((HEAD_DIM, d), w_out.dtype)
        slots.append(jnp.concatenate([blk, z] if g == 0 else [z, blk], axis=0))
    return jnp.concatenate(slots, axis=0).astype(MXU_DTYPE), w_out[Q_W:].astype(MXU_DTYPE)


def kernel(x, c, ctx, c_ctx, mod_w, mod_b, norm1_g, norm2_g, attn_in_w, attn_out_w, attn_sink, gmlp_w_s, gmlp_b_s, pool_w, pool_scale, peer_w_q, peer_sub_keys, peer_down, peer_up, final_g):
    batch, seq, d = x.shape
    n = batch * seq
    n_ctx = ctx.shape[1]
    depth = mod_w.shape[0]
    assert depth == 2 and d == D_MODEL
    assert seq % TM_DENSE == 0 and n_ctx % TM_DENSE == 0 and LANES % (2 * TB_GATHER) == 0
    assert n % (2 * TM_DENSE) == 0

    mod_rows = -(-(batch + 1) // SUBLANES) * SUBLANES
    cs = jnp.zeros((mod_rows, d), F32).at[:batch].set(c).at[batch].set(c_ctx)
    mods = _modulation(cs, mod_w, mod_b)
    seq_tiles = seq // TM_DENSE
    main_mod = lambda i: (i // seq_tiles, 0, 0)
    ctx_mod = lambda i: (batch, 0, 0)
    gather_mod = lambda i: (i // (seq // (2 * TB_GATHER)), 0, 0)

    x2 = x.reshape(n, d)
    row1 = lambda v: v.reshape(1, -1)

    w_ext = _inproj_weights(attn_in_w[0])
    cos, sin = _rope_tables(seq)
    m64 = jnp.kron(jnp.eye(GMLP_GROUPS, dtype=F32), jnp.full((GMLP_DIM, GMLP_DIM), 1.0 / GMLP_DIM, F32)
                   ).astype(MXU_DTYPE)
    q, k, v, u, gn = _inproj(x2, mods[0], main_mod, row1(norm1_g[0]), w_ext, cos, sin, m64, seq_tiles)
    ones = jnp.ones((n_ctx, LANES), F32)
    _, kx, vx, _, _ = _inproj(ctx.reshape(batch * n_ctx, d), mods[0], ctx_mod, row1(norm1_g[0]), w_ext,
                              ones, jnp.zeros_like(ones), m64, n_ctx // TM_DENSE)
    ws = gmlp_w_s[0].astype(MXU_DTYPE)
    bt = gmlp_b_s[0]
    bs = jnp.stack([jnp.concatenate([jnp.broadcast_to(bt[2 * j][:, None], (CHUNK, GMLP_DIM)),
                                     jnp.broadcast_to(bt[2 * j + 1][:, None], (CHUNK, GMLP_DIM))], axis=1)
                    for j in range(GMLP_GROUPS // 2)])
    attn, mix = _attn_gmlp(attn_sink[0], q, k, v, kx, vx, u, gn, ws, bs, batch, seq)
    wa, wm = _outproj_weights(attn_out_w[0])
    x1c, h2, h2c = _outproj(x2, attn, mix, wa, wm, mods[0], main_mod, row1(norm2_g[0]))
    chunked = lambda v: v.reshape(v.shape[:-1] + (d // LANES, LANES))
    xl = _peer(h2, h2c, x1c, chunked(mods[0][:, 5]), gather_mod, peer_w_q[0].T.astype(MXU_DTYPE),
               peer_sub_keys[0].astype(MXU_DTYPE), _pack_table(peer_down[0], peer_up[0]))

    x1c, h2, h2c = _pool(xl, mods[1], main_mod, row1(norm1_g[1]), row1(norm2_g[1]), pool_w[0].astype(MXU_DTYPE),
                         row1(pool_scale[0]), seq)
    xl = _peer(h2, h2c, x1c, chunked(mods[1][:, 5]), gather_mod, peer_w_q[1].T.astype(MXU_DTYPE),
               peer_sub_keys[1].astype(MXU_DTYPE), _pack_table(peer_down[1], peer_up[1]))
    return _final_norm(xl, row1(final_g)).reshape(batch, seq, d)
```

```python
import functools

import jax
import jax.numpy as jnp
from jax import lax
from jax.experimental import pallas as pl
from jax.experimental.pallas import tpu as pltpu

D_MODEL = 1024
GRID_W = 64
HEAD_DIM = 64
ATTN_HEADS = 8
KV_HEADS = 2
GQA_GROUP = ATTN_HEADS // KV_HEADS
WINDOW = 128
ATTN_BLOCK = 128
ROPE_BASE = 10000.0
GMLP_GROUPS = 8
GMLP_DIM = 64
CHUNK = 128
Q_W = ATTN_HEADS * HEAD_DIM
KV_W = KV_HEADS * HEAD_DIM
A_W = GMLP_GROUPS * GMLP_DIM
POOL_WINDOWS = (2, 4, 8, 16)
POOL_GROUPS = 4
POOL_DIM = D_MODEL // POOL_GROUPS
POOL_HALO = 8
PEER_HEADS = 8
PEER_NKEYS = 128
PEER_TOPK = 16
PEER_DKEY = 256
PEER_DHALF = PEER_DKEY // 2
PEER_SEL = PEER_HEADS * PEER_TOPK
N_MOD = 6
EPS = 1e-6

LANES = 128
SUBLANES = 8
QPAD_W = ATTN_HEADS * LANES
MXU_DTYPE = jnp.bfloat16
NEG = -1e30
F32 = jnp.float32

TM_DENSE = 256
TB_GATHER = 16
VMEM_LIMIT = 48 * 1024 * 1024


def _rms(x):
    return x * lax.rsqrt(jnp.mean(x * x, axis=-1, keepdims=True) + EPS)


def _dot(a, b):
    return jnp.dot(a, b, preferred_element_type=F32)


def _dot_nt(a, b):
    return lax.dot_general(a, b, (((1,), (1,)), ((), ())), preferred_element_type=F32)


def _mod_kernel(c_ref, w_ref, b_ref, o_ref):
    c = c_ref[...]
    s = c * jax.nn.sigmoid(c)
    o_ref[0] = jnp.dot(s, w_ref[0], preferred_element_type=F32, precision=lax.Precision.HIGHEST) + b_ref[0]


def _modulation(cs, mod_w, mod_b):
    depth, d, n6 = mod_w.shape
    r = cs.shape[0]
    tn = 512
    out = pl.pallas_call(
        _mod_kernel,
        grid=(depth, n6 // tn),
        in_specs=[pl.BlockSpec((r, d), lambda l, j: (0, 0)),
                  pl.BlockSpec((1, d, tn), lambda l, j: (l, 0, j)),
                  pl.BlockSpec((1, 1, tn), lambda l, j: (l, 0, j))],
        out_specs=pl.BlockSpec((1, r, tn), lambda l, j: (l, 0, j)),
        out_shape=jax.ShapeDtypeStruct((depth, r, n6), F32),
        name="adaln_modulation",
    )(cs, mod_w, mod_b.reshape(depth, 1, n6))
    return out.reshape(depth, r, N_MOD, d)


def _rope(t, cos, sin, first_half):
    partner = jnp.where(first_half, pltpu.roll(t, LANES - 16, 1), pltpu.roll(t, 16, 1))
    return t * cos + partner * sin


def _inproj_kernel(x_ref, mod_ref, g_ref, w_ref, cos_ref, sin_ref, m64_ref,
                   q_ref, k_ref, v_ref, u_ref, vn_ref):
    x = x_ref[...]
    sh = mod_ref[0, 0:1, :]
    sc = mod_ref[0, 1:2, :]
    h = _rms(x) * g_ref[...] * (1.0 + sc) + sh
    p = _dot(h.astype(MXU_DTYPE), w_ref[...])
    cos = cos_ref[...]
    sin = sin_ref[...]
    lane = lax.broadcasted_iota(jnp.int32, cos.shape, 1)
    first_half = (lane & 31) < 16
    scale = HEAD_DIM ** -0.5
    for s in range(ATTN_HEADS):
        t = _rope(p[:, s * LANES:(s + 1) * LANES], cos, sin, first_half)
        q_ref[:, s * LANES:(s + 1) * LANES] = (t * scale).astype(q_ref.dtype)
    o = QPAD_W
    k_ref[...] = _rope(p[:, o:o + KV_W], cos, sin, first_half).astype(k_ref.dtype)
    v_ref[...] = p[:, o + KV_W:o + 2 * KV_W].astype(v_ref.dtype)
    o += 2 * KV_W
    u_ref[...] = jax.nn.gelu(p[:, o:o + A_W]).astype(u_ref.dtype)
    vg = jax.nn.gelu(p[:, o + A_W:o + 2 * A_W])
    m64 = m64_ref[...]
    mu = _dot(vg.astype(MXU_DTYPE), m64)
    cen = vg - mu
    var = _dot((cen * cen).astype(MXU_DTYPE), m64)
    vn_ref[...] = (cen * lax.rsqrt(var + EPS)).astype(vn_ref.dtype)


def _inproj(x2, mod, mod_index, g, w_ext, cos, sin, m64, seq_tiles):
    n, d = x2.shape
    tm = TM_DENSE
    wcols = w_ext.shape[1]
    row = lambda i: (i, 0)
    const = lambda i: (0, 0)
    outs = [jax.ShapeDtypeStruct((n, QPAD_W), MXU_DTYPE),
            jax.ShapeDtypeStruct((n, KV_W), MXU_DTYPE),
            jax.ShapeDtypeStruct((n, KV_W), MXU_DTYPE),
            jax.ShapeDtypeStruct((n, A_W), MXU_DTYPE),
            jax.ShapeDtypeStruct((n, A_W), MXU_DTYPE)]
    return pl.pallas_call(
        _inproj_kernel,
        grid=(n // tm,),
        in_specs=[pl.BlockSpec((tm, d), row),
                  pl.BlockSpec((1, N_MOD, d), mod_index),
                  pl.BlockSpec((1, d), const),
                  pl.BlockSpec((d, wcols), const),
                  pl.BlockSpec((tm, LANES), lambda i: (i % seq_tiles, 0)),
                  pl.BlockSpec((tm, LANES), lambda i: (i % seq_tiles, 0)),
                  pl.BlockSpec((A_W, A_W), const)],
        out_specs=[pl.BlockSpec((tm, QPAD_W), row), pl.BlockSpec((tm, KV_W), row), pl.BlockSpec((tm, KV_W), row),
                   pl.BlockSpec((tm, A_W), row), pl.BlockSpec((tm, A_W), row)],
        out_shape=outs,
        compiler_params=pltpu.CompilerParams(vmem_limit_bytes=VMEM_LIMIT),
        name="inproj_rope",
    )(x2, mod, g, w_ext, cos, sin, m64)


def _attn_gmlp_kernel(sink_ref, q_ref, kp_ref, kc_ref, kn_ref, vp_ref, vc_ref, vn_ref, kx_ref, vx_ref,
                      u_ref, g_ref, ws_ref, bs_ref, attn_ref, mix_ref):
    n = pl.program_id(1)
    nb = pl.num_programs(1)
    rows = GQA_GROUP * ATTN_BLOCK
    ri = lax.broadcasted_iota(jnp.int32, (rows, ATTN_BLOCK), 0) & (ATTN_BLOCK - 1)
    ci = lax.broadcasted_iota(jnp.int32, (rows, ATTN_BLOCK), 1)
    mask_prev = jnp.logical_and(ci >= ri, n > 0)
    mask_next = jnp.logical_and(ci <= ri, n < nb - 1)
    rblk = lax.broadcasted_iota(jnp.int32, (rows, 1), 0) >> (ATTN_BLOCK.bit_length() - 1)
    kp, kc, kn, kx = kp_ref[...], kc_ref[...], kn_ref[...], kx_ref[...]
    vp, vc, vn, vx = vp_ref[...], vc_ref[...], vn_ref[...], vx_ref[...]
    for g in range(KV_HEADS):
        h0 = g * GQA_GROUP
        qg = jnp.concatenate([q_ref[:, (h0 + j) * LANES:(h0 + j + 1) * LANES] for j in range(GQA_GROUP)], axis=0)
        sink = jnp.full((rows, 1), sink_ref[h0], F32)
        for j in range(1, GQA_GROUP):
            sink = jnp.where(rblk == j, sink_ref[h0 + j], sink)
        s_p = jnp.where(mask_prev, _dot_nt(qg, kp), NEG)
        s_c = _dot_nt(qg, kc)
        s_n = jnp.where(mask_next, _dot_nt(qg, kn), NEG)
        s_x = _dot_nt(qg, kx)
        m = jnp.maximum(jnp.maximum(jnp.max(s_p, axis=1, keepdims=True), jnp.max(s_c, axis=1, keepdims=True)),
                        jnp.maximum(jnp.max(s_n, axis=1, keepdims=True), jnp.max(s_x, axis=1, keepdims=True)))
        m = jnp.maximum(m, sink)
        p_p, p_c, p_n, p_x = jnp.exp(s_p - m), jnp.exp(s_c - m), jnp.exp(s_n - m), jnp.exp(s_x - m)
        den = (jnp.sum(p_p, axis=1, keepdims=True) + jnp.sum(p_c, axis=1, keepdims=True)
               + jnp.sum(p_n, axis=1, keepdims=True) + jnp.sum(p_x, axis=1, keepdims=True) + jnp.exp(sink - m))
        o = (_dot(p_p.astype(MXU_DTYPE), vp) + _dot(p_c.astype(MXU_DTYPE), vc)
             + _dot(p_n.astype(MXU_DTYPE), vn) + _dot(p_x.astype(MXU_DTYPE), vx))
        o = o / den
        for j in range(GQA_GROUP):
            attn_ref[:, (h0 + j) * LANES:(h0 + j + 1) * LANES] = (
                o[j * ATTN_BLOCK:(j + 1) * ATTN_BLOCK].astype(attn_ref.dtype))
    lane = lax.broadcasted_iota(jnp.int32, (CHUNK, LANES), 1)
    for j in range(GMLP_GROUPS // 2):
        gp = g_ref[:, j * LANES:(j + 1) * LANES]
        mixed = jnp.where(lane < GMLP_DIM, _dot(ws_ref[2 * j], gp), _dot(ws_ref[2 * j + 1], gp)) + bs_ref[j]
        mix_ref[:, j * LANES:(j + 1) * LANES] = (u_ref[:, j * LANES:(j + 1) * LANES].astype(F32) * mixed
                                                 ).astype(mix_ref.dtype)


def _attn_gmlp(sink, q, k, v, kx, vx, u, gn, ws, bs, batch, seq):
    n = q.shape[0]
    nb = seq // ATTN_BLOCK
    cx = kx.shape[0] // batch
    blk = ATTN_BLOCK
    cur = lambda b, i: (b * nb + i, 0)
    prev = lambda b, i: (b * nb + jnp.maximum(i - 1, 0), 0)
    nxt = lambda b, i: (b * nb + jnp.minimum(i + 1, nb - 1), 0)
    ctx = lambda b, i: (b, 0)
    return pl.pallas_call(
        _attn_gmlp_kernel,
        grid=(batch, nb),
        in_specs=[pl.BlockSpec(memory_space=pltpu.SMEM),
                  pl.BlockSpec((blk, QPAD_W), cur),
                  pl.BlockSpec((blk, KV_W), prev), pl.BlockSpec((blk, KV_W), cur), pl.BlockSpec((blk, KV_W), nxt),
                  pl.BlockSpec((blk, KV_W), prev), pl.BlockSpec((blk, KV_W), cur), pl.BlockSpec((blk, KV_W), nxt),
                  pl.BlockSpec((cx, KV_W), ctx), pl.BlockSpec((cx, KV_W), ctx),
                  pl.BlockSpec((blk, A_W), cur), pl.BlockSpec((blk, A_W), cur),
                  pl.BlockSpec(ws.shape, lambda b, i: (0, 0, 0)),
                  pl.BlockSpec(bs.shape, lambda b, i: (0, 0, 0))],
        out_specs=[pl.BlockSpec((blk, QPAD_W), cur), pl.BlockSpec((blk, A_W), cur)],
        out_shape=[jax.ShapeDtypeStruct((n, QPAD_W), MXU_DTYPE), jax.ShapeDtypeStruct((n, A_W), MXU_DTYPE)],
        compiler_params=pltpu.CompilerParams(vmem_limit_bytes=VMEM_LIMIT),
        name="window_attn_gmlp",
    )(sink, q, k, k, k, v, v, v, kx, vx, u, gn, ws, bs)


def _residual_norm2(x, y, mod_ref, g2_ref, x1_ref, h2_ref):
    g1 = mod_ref[0, 2:3, :]
    sh2 = mod_ref[0, 3:4, :]
    sc2 = mod_ref[0, 4:5, :]
    x1 = x + g1 * y
    x1_ref[...] = x1
    h2_ref[...] = _rms(x1) * g2_ref[...] * (1.0 + sc2) + sh2


def _outproj_kernel(x_ref, a_ref, m_ref, wa_ref, wm_ref, mod_ref, g2_ref, x1_ref, h2_ref):
    y = _dot(a_ref[...], wa_ref[...]) + _dot(m_ref[...], wm_ref[...])
    _residual_norm2(x_ref[...], y, mod_ref, g2_ref, x1_ref, h2_ref)


def _mixer_outputs(n, d, tm):
    specs = [pl.BlockSpec((tm, d), lambda i: (i, 0)), pl.BlockSpec((tm, d), lambda i: (i, 0))]
    shapes = [jax.ShapeDtypeStruct((n, d), F32), jax.ShapeDtypeStruct((n, d), F32)]
    return specs, shapes


def _outproj(x2, attn, mix, wa, wm, mod, mod_index, g2):
    n, d = x2.shape
    tm = TM_DENSE
    row = lambda i: (i, 0)
    const = lambda i: (0, 0)
    out_specs, out_shape = _mixer_outputs(n, d, tm)
    return pl.pallas_call(
        _outproj_kernel,
        grid=(n // tm,),
        in_specs=[pl.BlockSpec((tm, d), row), pl.BlockSpec((tm, QPAD_W), row), pl.BlockSpec((tm, A_W), row),
                  pl.BlockSpec(wa.shape, const), pl.BlockSpec(wm.shape, const),
                  pl.BlockSpec((1, N_MOD, d), mod_index), pl.BlockSpec((1, d), const)],
        out_specs=out_specs,
        out_shape=out_shape,
        compiler_params=pltpu.CompilerParams(vmem_limit_bytes=VMEM_LIMIT),
        name="outproj_residual",
    )(x2, attn, mix, wa, wm, mod, g2)


def _pool_kernel(xp_ref, xc_ref, xn_ref, mod_ref, g1_ref, g2_ref, wp_ref, ps_ref, x1_ref, h2_ref, hs_ref,
                 *, seq_tiles, seq):
    i = pl.program_id(0)
    tm = xc_ref.shape[0]
    sh = mod_ref[0, 0:1, :]
    sc = mod_ref[0, 1:2, :]
    g1 = g1_ref[...]

    def norm(x):
        return _rms(x) * g1 * (1.0 + sc) + sh

    it = i % seq_tiles
    x = xc_ref[...]
    hs_ref[0:POOL_HALO, :] = jnp.where(it > 0, norm(xp_ref[...]), 0.0)
    hs_ref[POOL_HALO:POOL_HALO + tm, :] = norm(x)
    hs_ref[POOL_HALO + tm:, :] = jnp.where(it < seq_tiles - 1, norm(xn_ref[...]), 0.0)
    pos = it * tm + lax.broadcasted_iota(jnp.int32, (tm, 1), 0)
    y_parts = []
    for g, w in enumerate(POOL_WINDOWS):
        cols = slice(g * POOL_DIM, (g + 1) * POOL_DIM)
        tot = hs_ref[POOL_HALO - w // 2:POOL_HALO - w // 2 + tm, cols]
        for dlt in range(-w // 2 + 1, w // 2):
            tot = tot + hs_ref[POOL_HALO + dlt:POOL_HALO + dlt + tm, cols]
        cnt = (jnp.minimum(pos + w // 2, seq) - jnp.maximum(pos - w // 2, 0)).astype(F32)
        pooled = tot / cnt - hs_ref[POOL_HALO:POOL_HALO + tm, cols]
        y_parts.append(_dot(pooled.astype(MXU_DTYPE), wp_ref[g]))
    y = jnp.concatenate(y_parts, axis=1) * ps_ref[...]
    _residual_norm2(x, y, mod_ref, g2_ref, x1_ref, h2_ref)


def _pool(x2, mod, mod_index, g1, g2, wp, ps, seq):
    n, d = x2.shape
    tm = TM_DENSE
    seq_tiles = seq // tm
    hb = tm // POOL_HALO
    nh = n // POOL_HALO
    row = lambda i: (i, 0)
    const = lambda i: (0, 0)
    return pl.pallas_call(
        functools.partial(_pool_kernel, seq_tiles=seq_tiles, seq=seq),
        grid=(n // tm,),
        in_specs=[pl.BlockSpec((POOL_HALO, d), lambda i: (jnp.maximum(i * hb - 1, 0), 0)),
                  pl.BlockSpec((tm, d), row),
                  pl.BlockSpec((POOL_HALO, d), lambda i: (jnp.minimum((i + 1) * hb, nh - 1), 0)),
                  pl.BlockSpec((1, N_MOD, d), mod_index),
                  pl.BlockSpec((1, d), const), pl.BlockSpec((1, d), const),
                  pl.BlockSpec(wp.shape, lambda i: (0, 0, 0)), pl.BlockSpec((1, d), const)],
        out_specs=_mixer_outputs(n, d, tm)[0],
        out_shape=_mixer_outputs(n, d, tm)[1],
        scratch_shapes=[pltpu.VMEM((tm + 2 * POOL_HALO, d), F32)],
        compiler_params=pltpu.CompilerParams(vmem_limit_bytes=VMEM_LIMIT),
        name="pool_mixer_residual",
    )(x2, x2, x2, mod, g1, g2, wp, ps)


def _top16(s):
    t = s.shape[1]
    rows = lax.broadcasted_iota(jnp.int32, s.shape, 0).astype(F32)
    r16 = lax.broadcasted_iota(jnp.int32, (PEER_TOPK, t), 0)

    def body(r, carry):
        s, tv, ti = carry
        m = jnp.max(s, axis=0, keepdims=True)
        am = jnp.min(jnp.where(s == m, rows, float(PEER_NKEYS)), axis=0, keepdims=True)
        tv = jnp.where(r16 == r, m, tv)
        ti = jnp.where(r16 == r, am, ti)
        s = jnp.where(rows == am, -jnp.inf, s)
        return s, tv, ti

    _, tv, ti = lax.fori_loop(0, PEER_TOPK, body,
                              (s, jnp.zeros((PEER_TOPK, t), F32), jnp.zeros((PEER_TOPK, t), F32)))
    return tv, ti


def _combine16(tv1, ti1, tv2, ti2):
    k = PEER_TOPK
    t = tv1.shape[1]
    half = k // 2
    sv = [tv1[0:1] + tv2]
    ev = [ti1[0:1] * PEER_NKEYS + ti2]
    for a in range(1, half):
        sv.append(tv1[a:a + 1] + tv2[0:half])
        ev.append(ti1[a:a + 1] * PEER_NKEYS + ti2[0:half])
    sv.append(tv1[half:k] + tv2[0:1])
    ev.append(ti1[half:k] * PEER_NKEYS + ti2[0:1])
    cand = jnp.concatenate(sv, axis=0)
    cexp = jnp.concatenate(ev, axis=0)
    nrow = cand.shape[0]
    r = lax.broadcasted_iota(jnp.int32, (nrow, t), 0)
    mid = r - k
    hbits = half.bit_length() - 1
    flat = jnp.where(r < k, r,
                     jnp.where(r < k + (half - 1) * half, ((mid >> hbits) + 1) * k + (mid & (half - 1)),
                               (r - (k + (half - 1) * half) + half) * k)).astype(F32)
    r16 = lax.broadcasted_iota(jnp.int32, (k, t), 0)

    def body(j, carry):
        cand, bs, be = carry
        m = jnp.max(cand, axis=0, keepdims=True)
        sel = jnp.min(jnp.where(cand == m, flat, float(k * k)), axis=0, keepdims=True)
        hit = flat == sel
        e = jnp.max(jnp.where(hit, cexp, -1.0), axis=0, keepdims=True)
        bs = jnp.where(r16 == j, m, bs)
        be = jnp.where(r16 == j, e, be)
        cand = jnp.where(hit, -jnp.inf, cand)
        return cand, bs, be

    _, bs, be = lax.fori_loop(0, k, body, (cand, jnp.zeros((k, t), F32), jnp.zeros((k, t), F32)))
    w = jnp.exp(bs - bs[0:1])
    gate = w / jnp.sum(w, axis=0, keepdims=True)
    return gate, be.astype(jnp.int32)


def _router_kernel(h2_ref, wq_ref, sk_ref, idx_ref, gate_ref, st_ref):
    hb = h2_ref[...].astype(MXU_DTYPE)
    qt = _dot_nt(wq_ref[...], hb)
    for hp in range(PEER_HEADS * 2):
        blk = qt[hp * PEER_DHALF:(hp + 1) * PEER_DHALF].astype(MXU_DTYPE)
        st_ref[hp] = _dot(sk_ref[hp % 2], blk)
    for h in range(PEER_HEADS):
        tv1, ti1 = _top16(st_ref[2 * h])
        tv2, ti2 = _top16(st_ref[2 * h + 1])
        gate, be = _combine16(tv1, ti1, tv2, ti2)
        idx_ref[h * PEER_TOPK:(h + 1) * PEER_TOPK, :] = be
        gate_ref[h * PEER_TOPK:(h + 1) * PEER_TOPK, :] = gate


def _router(h2, wq_t, sk):
    n, d = h2.shape
    tm = TM_DENSE
    col = lambda i: (0, i)
    return pl.pallas_call(
        _router_kernel,
        grid=(n // tm,),
        in_specs=[pl.BlockSpec((tm, d), lambda i: (i, 0)),
                  pl.BlockSpec(wq_t.shape, lambda i: (0, 0)),
                  pl.BlockSpec(sk.shape, lambda i: (0, 0, 0))],
        out_specs=[pl.BlockSpec((PEER_SEL, tm), col), pl.BlockSpec((PEER_SEL, tm), col)],
        out_shape=[jax.ShapeDtypeStruct((PEER_SEL, n), jnp.int32), jax.ShapeDtypeStruct((PEER_SEL, n), F32)],
        scratch_shapes=[pltpu.VMEM((PEER_HEADS * 2, PEER_NKEYS, tm), F32)],
        compiler_params=pltpu.CompilerParams(vmem_limit_bytes=VMEM_LIMIT),
        name="peer_router",
    )(h2, wq_t, sk)


def _gather_kernel(idx_hbm, gate_ref, h2_ref, x1_ref, mod_ref, rep_ref, rept_ref, tab_hbm, out_ref,
                   idx_smem, buf, m_ref, isem, gsem, *, tb):
    i = pl.program_id(0)
    nsteps = pl.num_programs(0)
    nchunk = D_MODEL // LANES
    tile_rows = 2 * nchunk
    tok_rows = PEER_SEL * tile_rows

    def idx_copy(block, sl):
        return pltpu.make_async_copy(idx_hbm.at[pl.ds(block * tb, tb), :], idx_smem.at[sl], isem.at[sl])

    def row_copy(sl, t, s, e):
        return pltpu.make_async_copy(tab_hbm.at[e],
                                     buf.at[sl, pl.ds(t * tok_rows + s * tile_rows, tile_rows), :],
                                     gsem.at[sl * tb + t])

    def issue(sl, t, sels):
        for s in sels:
            row_copy(sl, t, s, idx_smem[sl, t, s]).start(priority=s % 2)

    def wait_token(sl, t):
        for s in range(PEER_SEL):
            row_copy(sl, t, s, 0).wait()

    @pl.when(i == 0)
    def _():
        c = idx_copy(0, 0)
        c.start()
        c.wait()

        def tok(t, carry):
            issue(0, t, range(PEER_SEL))
            return carry
        lax.fori_loop(0, tb, tok, 0)
        idx_copy(1, 1).start()

    g2 = mod_ref[0, 5:6, :]
    lane16 = lax.broadcasted_iota(jnp.int32, (nchunk, tok_rows), 1) & (tile_rows - 1)
    chunk = lax.broadcasted_iota(jnp.int32, (nchunk, tok_rows), 0)
    down_diag = (lane16 == chunk).astype(F32)
    up_diag = (lane16 == chunk + nchunk).astype(F32)
    sub = lax.broadcasted_iota(jnp.int32, (SUBLANES, LANES), 0)
    zpad = jnp.zeros((nchunk, LANES), F32)
    half = PEER_SEL // 2

    mxu_rows = 256
    nsplit = tok_rows // mxu_rows
    group_copies = SUBLANES * PEER_SEL
    mid_split = group_copies // 16
    per_split = (group_copies - 4 * mid_split) // (2 * SUBLANES * nsplit)

    def tiles(sl, t, k):
        return buf[sl, t * tok_rows + k * mxu_rows:t * tok_rows + (k + 1) * mxu_rows, :]

    def phase(sl):
        for t8 in range(tb // SUBLANES):
            r0 = sl * tb + t8 * SUBLANES
            for j in range(SUBLANES):
                wait_token(sl, t8 * SUBLANES + j)
            todo = [(t8 * SUBLANES + j, s) for j in range(SUBLANES) for s in range(PEER_SEL)]

            def take(n):
                for t, s in todo[:n]:
                    row_copy(1 - sl, t, s, idx_smem[1 - sl, t, s]).start(priority=s % 2)
                del todo[:n]

            for j in range(SUBLANES):
                t = t8 * SUBLANES + j
                xrow = h2_ref[r0 + j:r0 + j + 1, :]
                x8 = zpad
                for c in range(nchunk):
                    x8 = jnp.where(sub == c, xrow[:, c * LANES:(c + 1) * LANES], x8)
                x16 = jnp.concatenate([x8, zpad], axis=0).astype(MXU_DTYPE)
                for k in range(nsplit):
                    cols = slice(k * mxu_rows, (k + 1) * mxu_rows)
                    part = _dot_nt(x16, tiles(sl, t, k))[0:nchunk] * down_diag[:, cols]
                    m_ref[j * nchunk:(j + 1) * nchunk, cols] = part
                    take(per_split)
            m_all = m_ref[...]
            m_hi = m_all.astype(MXU_DTYPE)
            m_lo = (m_all - m_hi.astype(F32)).astype(MXU_DTYPE)
            take(mid_split)
            e2 = _dot(jnp.concatenate([m_hi, m_lo], axis=0), rept_ref[...])
            e = e2[0:SUBLANES * nchunk] + e2[SUBLANES * nchunk:]
            take(mid_split)
            d = jnp.zeros((SUBLANES, PEER_SEL), F32)
            for j in range(SUBLANES):
                d = jnp.where(sub == j, jnp.sum(e[j * nchunk:(j + 1) * nchunk], axis=0, keepdims=True), d)
            a = gate_ref[r0:r0 + SUBLANES, :] * jax.nn.gelu(d)
            take(mid_split)
            a_exp = _dot(a.astype(MXU_DTYPE), rep_ref[...])
            take(mid_split)
            for j in range(SUBLANES):
                t = t8 * SUBLANES + j
                v = jnp.broadcast_to(a_exp[j:j + 1], (nchunk, tok_rows)) * up_diag
                v16 = jnp.concatenate([v, jnp.zeros_like(v)], axis=0).astype(MXU_DTYPE)
                y = jnp.zeros((tile_rows, LANES), F32)
                for k in range(nsplit):
                    y = y + _dot(v16[:, k * mxu_rows:(k + 1) * mxu_rows], tiles(sl, t, k))
                    take(per_split)
                for c in range(nchunk):
                    cols = slice(c * LANES, (c + 1) * LANES)
                    out_ref[r0 + j:r0 + j + 1, cols] = x1_ref[r0 + j:r0 + j + 1, cols] + g2[:, cols] * y[c:c + 1, :]
            take(len(todo))

    more = i + 1 < nsteps
    idx_copy(2 * i + 1, 1).wait()

    @pl.when(more)
    def _():
        idx_copy(2 * i + 2, 0).start()

    phase(0)

    @pl.when(more)
    def _():
        idx_copy(2 * i + 2, 0).wait()
        idx_copy(2 * i + 3, 1).start()

    phase(1)

    @pl.when(jnp.logical_not(more))
    def _():
        for t in range(tb):
            wait_token(0, t)


def _gather(idx_tm, gate_tm, h2, x1, mod, mod_index, table):
    n, d = x1.shape
    nchunk = d // LANES
    tb = TB_GATHER
    step = 2 * tb
    tile_rows = 2 * nchunk
    tok_rows = PEER_SEL * tile_rows
    rep = (jnp.arange(tok_rows)[None, :] // tile_rows == jnp.arange(PEER_SEL)[:, None]).astype(MXU_DTYPE)
    row = lambda i: (i, 0)
    const = lambda i: (0, 0)
    return pl.pallas_call(
        functools.partial(_gather_kernel, tb=tb),
        grid=(n // step,),
        in_specs=[pl.BlockSpec(memory_space=pl.ANY),
                  pl.BlockSpec((step, PEER_SEL), row),
                  pl.BlockSpec((step, d), row), pl.BlockSpec((step, d), row),
                  pl.BlockSpec((1, N_MOD, d), mod_index),
                  pl.BlockSpec((PEER_SEL, tok_rows), const),
                  pl.BlockSpec((tok_rows, PEER_SEL), const),
                  pl.BlockSpec(memory_space=pl.ANY)],
        out_specs=pl.BlockSpec((step, d), row),
        out_shape=jax.ShapeDtypeStruct((n, d), F32),
        scratch_shapes=[pltpu.SMEM((2, tb, PEER_SEL), jnp.int32),
                        pltpu.VMEM((2, tb * tok_rows, LANES), MXU_DTYPE),
                        pltpu.VMEM((SUBLANES * nchunk, tok_rows), F32),
                        pltpu.SemaphoreType.DMA((2,)),
                        pltpu.SemaphoreType.DMA((2 * tb,))],
        compiler_params=pltpu.CompilerParams(dimension_semantics=("arbitrary",), vmem_limit_bytes=VMEM_LIMIT),
        name="peer_gather",
    )(idx_tm, gate_tm, h2, x1, mod, rep, rep.T, table)


def _peer(h2, x1, mod, mod_index, wq_t, sk, table):
    idx, gate = _router(h2, wq_t, sk)
    return _gather(idx.T, gate.T, h2, x1, mod, mod_index, table)


def _final_norm_kernel(x_ref, g_ref, o_ref):
    o_ref[...] = _rms(x_ref[...]) * g_ref[...]


def _final_norm(x2, g):
    n, d = x2.shape
    tm = 2 * TM_DENSE
    return pl.pallas_call(
        _final_norm_kernel,
        grid=(n // tm,),
        in_specs=[pl.BlockSpec((tm, d), lambda i: (i, 0)), pl.BlockSpec((1, d), lambda i: (0, 0))],
        out_specs=pl.BlockSpec((tm, d), lambda i: (i, 0)),
        out_shape=jax.ShapeDtypeStruct((n, d), F32),
        name="final_rmsnorm",
    )(x2, g)


def _pack_table(down, up):
    e, d = down.shape
    chunks = (e, d // LANES, LANES)
    return jnp.concatenate([down.reshape(chunks), up.reshape(chunks)], axis=1).astype(MXU_DTYPE)


def _rope_tables(seq):
    quarter = HEAD_DIM // 4
    inv = jnp.power(ROPE_BASE, -jnp.arange(quarter, dtype=F32) / quarter)
    t = jnp.arange(seq)
    row = (t // GRID_W).astype(F32)
    col = (t % GRID_W).astype(F32)
    ang_r = row[:, None] * inv[None, :]
    ang_c = col[:, None] * inv[None, :]
    cos = jnp.concatenate([jnp.cos(ang_r)] * 2 + [jnp.cos(ang_c)] * 2, axis=1)
    sin = jnp.concatenate([-jnp.sin(ang_r), jnp.sin(ang_r), -jnp.sin(ang_c), jnp.sin(ang_c)], axis=1)
    reps = LANES // HEAD_DIM
    return jnp.tile(cos, (1, reps)), jnp.tile(sin, (1, reps))


def _inproj_weights(w_in):
    d = w_in.shape[0]
    slots = []
    for h in range(ATTN_HEADS):
        g = h // GQA_GROUP
        blk = w_in[:, h * HEAD_DIM:(h + 1) * HEAD_DIM]
        z = jnp.zeros((d, HEAD_DIM), w_in.dtype)
        slots.append(jnp.concatenate([blk, z] if g == 0 else [z, blk], axis=1))
    return jnp.concatenate(slots + [w_in[:, Q_W:]], axis=1).astype(MXU_DTYPE)


def _outproj_weights(w_out):
    d = w_out.shape[1]
    slots = []
    for h in range(ATTN_HEADS):
        g = h // GQA_GROUP
        blk = w_out[h * HEAD_DIM:(h + 1) * HEAD_DIM]
        z = jnp.zeros((HEAD_DIM, d), w_out.dtype)
        slots.append(jnp.concatenate([blk, z] if g == 0 else [z, blk], axis=0))
    return jnp.concatenate(slots, axis=0).astype(MXU_DTYPE), w_out[Q_W:].astype(MXU_DTYPE)


def kernel(x, c, ctx, c_ctx, mod_w, mod_b, norm1_g, norm2_g, attn_in_w, attn_out_w, attn_sink, gmlp_w_s, gmlp_b_s, pool_w, pool_scale, peer_w_q, peer_sub_keys, peer_down, peer_up, final_g):
    batch, seq, d = x.shape
    n = batch * seq
    n_ctx = ctx.shape[1]
    depth = mod_w.shape[0]
    assert depth == 2 and d == D_MODEL
    assert seq % TM_DENSE == 0 and n_ctx % TM_DENSE == 0 and LANES % (2 * TB_GATHER) == 0
    assert n % (2 * TM_DENSE) == 0

    mod_rows = -(-(batch + 1) // SUBLANES) * SUBLANES
    cs = jnp.zeros((mod_rows, d), F32).at[:batch].set(c).at[batch].set(c_ctx)
    mods = _modulation(cs, mod_w, mod_b)
    seq_tiles = seq // TM_DENSE
    main_mod = lambda i: (i // seq_tiles, 0, 0)
    ctx_mod = lambda i: (batch, 0, 0)
    gather_mod = lambda i: (i // (seq // (2 * TB_GATHER)), 0, 0)

    x2 = x.reshape(n, d)
    row1 = lambda v: v.reshape(1, -1)

    w_ext = _inproj_weights(attn_in_w[0])
    cos, sin = _rope_tables(seq)
    m64 = jnp.kron(jnp.eye(GMLP_GROUPS, dtype=F32), jnp.full((GMLP_DIM, GMLP_DIM), 1.0 / GMLP_DIM, F32)
                   ).astype(MXU_DTYPE)
    q, k, v, u, gn = _inproj(x2, mods[0], main_mod, row1(norm1_g[0]), w_ext, cos, sin, m64, seq_tiles)
    ones = jnp.ones((n_ctx, LANES), F32)
    _, kx, vx, _, _ = _inproj(ctx.reshape(batch * n_ctx, d), mods[0], ctx_mod, row1(norm1_g[0]), w_ext,
                              ones, jnp.zeros_like(ones), m64, n_ctx // TM_DENSE)
    ws = gmlp_w_s[0].astype(MXU_DTYPE)
    bt = gmlp_b_s[0]
    bs = jnp.stack([jnp.concatenate([jnp.broadcast_to(bt[2 * j][:, None], (CHUNK, GMLP_DIM)),
                                     jnp.broadcast_to(bt[2 * j + 1][:, None], (CHUNK, GMLP_DIM))], axis=1)
                    for j in range(GMLP_GROUPS // 2)])
    attn, mix = _attn_gmlp(attn_sink[0], q, k, v, kx, vx, u, gn, ws, bs, batch, seq)
    wa, wm = _outproj_weights(attn_out_w[0])
    x1, h2 = _outproj(x2, attn, mix, wa, wm, mods[0], main_mod, row1(norm2_g[0]))
    xl = _peer(h2, x1, mods[0], gather_mod, peer_w_q[0].T.astype(MXU_DTYPE),
               peer_sub_keys[0].astype(MXU_DTYPE), _pack_table(peer_down[0], peer_up[0]))

    x1, h2 = _pool(xl, mods[1], main_mod, row1(norm1_g[1]), row1(norm2_g[1]), pool_w[0].astype(MXU_DTYPE),
                   row1(pool_scale[0]), seq)
    xl = _peer(h2, x1, mods[1], gather_mod, peer_w_q[1].T.astype(MXU_DTYPE),
               peer_sub_keys[1].astype(MXU_DTYPE), _pack_table(peer_down[1], peer_up[1]))
    return _final_norm(xl, row1(final_g)).reshape(batch, seq, d)
```

```python
import functools

import jax
import jax.numpy as jnp
from jax import lax
from jax.experimental import pallas as pl
from jax.experimental.pallas import tpu as pltpu

D_MODEL = 1024
GRID_W = 64
HEAD_DIM = 64
ATTN_HEADS = 8
KV_HEADS = 2
GQA_GROUP = ATTN_HEADS // KV_HEADS
WINDOW = 128
ATTN_BLOCK = 128
ROPE_BASE = 10000.0
GMLP_GROUPS = 8
GMLP_DIM = 64
CHUNK = 128
Q_W = ATTN_HEADS * HEAD_DIM
KV_W = KV_HEADS * HEAD_DIM
A_W = GMLP_GROUPS * GMLP_DIM
POOL_WINDOWS = (2, 4, 8, 16)
POOL_GROUPS = 4
POOL_DIM = D_MODEL // POOL_GROUPS
POOL_HALO = 8
PEER_HEADS = 8
PEER_NKEYS = 128
PEER_TOPK = 16
PEER_DKEY = 256
PEER_DHALF = PEER_DKEY // 2
PEER_SEL = PEER_HEADS * PEER_TOPK
N_MOD = 6
EPS = 1e-6

LANES = 128
SUBLANES = 8
QPAD_W = ATTN_HEADS * LANES
MXU_DTYPE = jnp.bfloat16
NEG = -1e30
F32 = jnp.float32

TM_DENSE = 256
TB_GATHER = 16
VMEM_LIMIT = 48 * 1024 * 1024


def _rms(x):
    return x * lax.rsqrt(jnp.mean(x * x, axis=-1, keepdims=True) + EPS)


def _dot(a, b):
    return jnp.dot(a, b, preferred_element_type=F32)


def _dot_nt(a, b):
    return lax.dot_general(a, b, (((1,), (1,)), ((), ())), preferred_element_type=F32)


def _mod_kernel(c_ref, w_ref, b_ref, o_ref):
    c = c_ref[...]
    s = c * jax.nn.sigmoid(c)
    o_ref[0] = jnp.dot(s, w_ref[0], preferred_element_type=F32, precision=lax.Precision.HIGHEST) + b_ref[0]


def _modulation(cs, mod_w, mod_b):
    depth, d, n6 = mod_w.shape
    r = cs.shape[0]
    tn = 512
    out = pl.pallas_call(
        _mod_kernel,
        grid=(depth, n6 // tn),
        in_specs=[pl.BlockSpec((r, d), lambda l, j: (0, 0)),
                  pl.BlockSpec((1, d, tn), lambda l, j: (l, 0, j)),
                  pl.BlockSpec((1, 1, tn), lambda l, j: (l, 0, j))],
        out_specs=pl.BlockSpec((1, r, tn), lambda l, j: (l, 0, j)),
        out_shape=jax.ShapeDtypeStruct((depth, r, n6), F32),
        name="adaln_modulation",
    )(cs, mod_w, mod_b.reshape(depth, 1, n6))
    return out.reshape(depth, r, N_MOD, d)


def _rope(t, cos, sin, first_half):
    partner = jnp.where(first_half, pltpu.roll(t, LANES - 16, 1), pltpu.roll(t, 16, 1))
    return t * cos + partner * sin


def _inproj_kernel(x_ref, mod_ref, g_ref, w_ref, cos_ref, sin_ref, m64_ref,
                   q_ref, k_ref, v_ref, u_ref, vn_ref):
    x = x_ref[...]
    sh = mod_ref[0, 0:1, :]
    sc = mod_ref[0, 1:2, :]
    h = _rms(x) * g_ref[...] * (1.0 + sc) + sh
    p = _dot(h.astype(MXU_DTYPE), w_ref[...])
    cos = cos_ref[...]
    sin = sin_ref[...]
    lane = lax.broadcasted_iota(jnp.int32, cos.shape, 1)
    first_half = (lane & 31) < 16
    scale = HEAD_DIM ** -0.5
    for s in range(ATTN_HEADS):
        t = _rope(p[:, s * LANES:(s + 1) * LANES], cos, sin, first_half)
        q_ref[:, s * LANES:(s + 1) * LANES] = (t * scale).astype(q_ref.dtype)
    o = QPAD_W
    k_ref[...] = _rope(p[:, o:o + KV_W], cos, sin, first_half).astype(k_ref.dtype)
    v_ref[...] = p[:, o + KV_W:o + 2 * KV_W].astype(v_ref.dtype)
    o += 2 * KV_W
    u_ref[...] = jax.nn.gelu(p[:, o:o + A_W]).astype(u_ref.dtype)
    vg = jax.nn.gelu(p[:, o + A_W:o + 2 * A_W])
    m64 = m64_ref[...]
    mu = _dot(vg.astype(MXU_DTYPE), m64)
    cen = vg - mu
    var = _dot((cen * cen).astype(MXU_DTYPE), m64)
    vn_ref[...] = (cen * lax.rsqrt(var + EPS)).astype(vn_ref.dtype)


def _inproj(x2, mod, mod_index, g, w_ext, cos, sin, m64, seq_tiles):
    n, d = x2.shape
    tm = TM_DENSE
    wcols = w_ext.shape[1]
    row = lambda i: (i, 0)
    const = lambda i: (0, 0)
    outs = [jax.ShapeDtypeStruct((n, QPAD_W), MXU_DTYPE),
            jax.ShapeDtypeStruct((n, KV_W), MXU_DTYPE),
            jax.ShapeDtypeStruct((n, KV_W), MXU_DTYPE),
            jax.ShapeDtypeStruct((n, A_W), MXU_DTYPE),
            jax.ShapeDtypeStruct((n, A_W), MXU_DTYPE)]
    return pl.pallas_call(
        _inproj_kernel,
        grid=(n // tm,),
        in_specs=[pl.BlockSpec((tm, d), row),
                  pl.BlockSpec((1, N_MOD, d), mod_index),
                  pl.BlockSpec((1, d), const),
                  pl.BlockSpec((d, wcols), const),
                  pl.BlockSpec((tm, LANES), lambda i: (i % seq_tiles, 0)),
                  pl.BlockSpec((tm, LANES), lambda i: (i % seq_tiles, 0)),
                  pl.BlockSpec((A_W, A_W), const)],
        out_specs=[pl.BlockSpec((tm, QPAD_W), row), pl.BlockSpec((tm, KV_W), row), pl.BlockSpec((tm, KV_W), row),
                   pl.BlockSpec((tm, A_W), row), pl.BlockSpec((tm, A_W), row)],
        out_shape=outs,
        compiler_params=pltpu.CompilerParams(vmem_limit_bytes=VMEM_LIMIT),
        name="inproj_rope",
    )(x2, mod, g, w_ext, cos, sin, m64)


def _attn_gmlp_kernel(sink_ref, q_ref, kp_ref, kc_ref, kn_ref, vp_ref, vc_ref, vn_ref, kx_ref, vx_ref,
                      u_ref, g_ref, ws_ref, bs_ref, attn_ref, mix_ref):
    n = pl.program_id(1)
    nb = pl.num_programs(1)
    rows = GQA_GROUP * ATTN_BLOCK
    ri = lax.broadcasted_iota(jnp.int32, (rows, ATTN_BLOCK), 0) & (ATTN_BLOCK - 1)
    ci = lax.broadcasted_iota(jnp.int32, (rows, ATTN_BLOCK), 1)
    mask_prev = jnp.logical_and(ci >= ri, n > 0)
    mask_next = jnp.logical_and(ci <= ri, n < nb - 1)
    rblk = lax.broadcasted_iota(jnp.int32, (rows, 1), 0) >> (ATTN_BLOCK.bit_length() - 1)
    kp, kc, kn, kx = kp_ref[...], kc_ref[...], kn_ref[...], kx_ref[...]
    vp, vc, vn, vx = vp_ref[...], vc_ref[...], vn_ref[...], vx_ref[...]
    for g in range(KV_HEADS):
        h0 = g * GQA_GROUP
        qg = jnp.concatenate([q_ref[:, (h0 + j) * LANES:(h0 + j + 1) * LANES] for j in range(GQA_GROUP)], axis=0)
        sink = jnp.full((rows, 1), sink_ref[h0], F32)
        for j in range(1, GQA_GROUP):
            sink = jnp.where(rblk == j, sink_ref[h0 + j], sink)
        s_p = jnp.where(mask_prev, _dot_nt(qg, kp), NEG)
        s_c = _dot_nt(qg, kc)
        s_n = jnp.where(mask_next, _dot_nt(qg, kn), NEG)
        s_x = _dot_nt(qg, kx)
        m = jnp.maximum(jnp.maximum(jnp.max(s_p, axis=1, keepdims=True), jnp.max(s_c, axis=1, keepdims=True)),
                        jnp.maximum(jnp.max(s_n, axis=1, keepdims=True), jnp.max(s_x, axis=1, keepdims=True)))
        m = jnp.maximum(m, sink)
        p_p, p_c, p_n, p_x = jnp.exp(s_p - m), jnp.exp(s_c - m), jnp.exp(s_n - m), jnp.exp(s_x - m)
        den = (jnp.sum(p_p, axis=1, keepdims=True) + jnp.sum(p_c, axis=1, keepdims=True)
               + jnp.sum(p_n, axis=1, keepdims=True) + jnp.sum(p_x, axis=1, keepdims=True) + jnp.exp(sink - m))
        o = (_dot(p_p.astype(MXU_DTYPE), vp) + _dot(p_c.astype(MXU_DTYPE), vc)
             + _dot(p_n.astype(MXU_DTYPE), vn) + _dot(p_x.astype(MXU_DTYPE), vx))
        o = o / den
        for j in range(GQA_GROUP):
            attn_ref[:, (h0 + j) * LANES:(h0 + j + 1) * LANES] = (
                o[j * ATTN_BLOCK:(j + 1) * ATTN_BLOCK].astype(attn_ref.dtype))
    lane = lax.broadcasted_iota(jnp.int32, (CHUNK, LANES), 1)
    for j in range(GMLP_GROUPS // 2):
        gp = g_ref[:, j * LANES:(j + 1) * LANES]
        mixed = jnp.where(lane < GMLP_DIM, _dot(ws_ref[2 * j], gp), _dot(ws_ref[2 * j + 1], gp)) + bs_ref[j]
        mix_ref[:, j * LANES:(j + 1) * LANES] = (u_ref[:, j * LANES:(j + 1) * LANES].astype(F32) * mixed
                                                 ).astype(mix_ref.dtype)


def _attn_gmlp(sink, q, k, v, kx, vx, u, gn, ws, bs, batch, seq):
    n = q.shape[0]
    nb = seq // ATTN_BLOCK
    cx = kx.shape[0] // batch
    blk = ATTN_BLOCK
    cur = lambda b, i: (b * nb + i, 0)
    prev = lambda b, i: (b * nb + jnp.maximum(i - 1, 0), 0)
    nxt = lambda b, i: (b * nb + jnp.minimum(i + 1, nb - 1), 0)
    ctx = lambda b, i: (b, 0)
    return pl.pallas_call(
        _attn_gmlp_kernel,
        grid=(batch, nb),
        in_specs=[pl.BlockSpec(memory_space=pltpu.SMEM),
                  pl.BlockSpec((blk, QPAD_W), cur),
                  pl.BlockSpec((blk, KV_W), prev), pl.BlockSpec((blk, KV_W), cur), pl.BlockSpec((blk, KV_W), nxt),
                  pl.BlockSpec((blk, KV_W), prev), pl.BlockSpec((blk, KV_W), cur), pl.BlockSpec((blk, KV_W), nxt),
                  pl.BlockSpec((cx, KV_W), ctx), pl.BlockSpec((cx, KV_W), ctx),
                  pl.BlockSpec((blk, A_W), cur), pl.BlockSpec((blk, A_W), cur),
                  pl.BlockSpec(ws.shape, lambda b, i: (0, 0, 0)),
                  pl.BlockSpec(bs.shape, lambda b, i: (0, 0, 0))],
        out_specs=[pl.BlockSpec((blk, QPAD_W), cur), pl.BlockSpec((blk, A_W), cur)],
        out_shape=[jax.ShapeDtypeStruct((n, QPAD_W), MXU_DTYPE), jax.ShapeDtypeStruct((n, A_W), MXU_DTYPE)],
        compiler_params=pltpu.CompilerParams(vmem_limit_bytes=VMEM_LIMIT),
        name="window_attn_gmlp",
    )(sink, q, k, k, k, v, v, v, kx, vx, u, gn, ws, bs)


def _residual_norm2(x, y, mod_ref, g2_ref, x1_ref, h2_ref):
    g1 = mod_ref[0, 2:3, :]
    sh2 = mod_ref[0, 3:4, :]
    sc2 = mod_ref[0, 4:5, :]
    x1 = x + g1 * y
    x1_ref[...] = x1
    h2_ref[...] = _rms(x1) * g2_ref[...] * (1.0 + sc2) + sh2


def _outproj_kernel(x_ref, a_ref, m_ref, wa_ref, wm_ref, mod_ref, g2_ref, x1_ref, h2_ref):
    y = _dot(a_ref[...], wa_ref[...]) + _dot(m_ref[...], wm_ref[...])
    _residual_norm2(x_ref[...], y, mod_ref, g2_ref, x1_ref, h2_ref)


def _mixer_outputs(n, d, tm):
    specs = [pl.BlockSpec((tm, d), lambda i: (i, 0)), pl.BlockSpec((tm, d), lambda i: (i, 0))]
    shapes = [jax.ShapeDtypeStruct((n, d), F32), jax.ShapeDtypeStruct((n, d), F32)]
    return specs, shapes


def _outproj(x2, attn, mix, wa, wm, mod, mod_index, g2):
    n, d = x2.shape
    tm = TM_DENSE
    row = lambda i: (i, 0)
    const = lambda i: (0, 0)
    out_specs, out_shape = _mixer_outputs(n, d, tm)
    return pl.pallas_call(
        _outproj_kernel,
        grid=(n // tm,),
        in_specs=[pl.BlockSpec((tm, d), row), pl.BlockSpec((tm, QPAD_W), row), pl.BlockSpec((tm, A_W), row),
                  pl.BlockSpec(wa.shape, const), pl.BlockSpec(wm.shape, const),
                  pl.BlockSpec((1, N_MOD, d), mod_index), pl.BlockSpec((1, d), const)],
        out_specs=out_specs,
        out_shape=out_shape,
        compiler_params=pltpu.CompilerParams(vmem_limit_bytes=VMEM_LIMIT),
        name="outproj_residual",
    )(x2, attn, mix, wa, wm, mod, g2)


def _pool_kernel(xp_ref, xc_ref, xn_ref, mod_ref, g1_ref, g2_ref, wp_ref, ps_ref, x1_ref, h2_ref, hs_ref,
                 *, seq_tiles, seq):
    i = pl.program_id(0)
    tm = xc_ref.shape[0]
    sh = mod_ref[0, 0:1, :]
    sc = mod_ref[0, 1:2, :]
    g1 = g1_ref[...]

    def norm(x):
        return _rms(x) * g1 * (1.0 + sc) + sh

    it = i % seq_tiles
    x = xc_ref[...]
    hs_ref[0:POOL_HALO, :] = jnp.where(it > 0, norm(xp_ref[...]), 0.0)
    hs_ref[POOL_HALO:POOL_HALO + tm, :] = norm(x)
    hs_ref[POOL_HALO + tm:, :] = jnp.where(it < seq_tiles - 1, norm(xn_ref[...]), 0.0)
    pos = it * tm + lax.broadcasted_iota(jnp.int32, (tm, 1), 0)
    y_parts = []
    for g, w in enumerate(POOL_WINDOWS):
        cols = slice(g * POOL_DIM, (g + 1) * POOL_DIM)
        tot = hs_ref[POOL_HALO - w // 2:POOL_HALO - w // 2 + tm, cols]
        for dlt in range(-w // 2 + 1, w // 2):
            tot = tot + hs_ref[POOL_HALO + dlt:POOL_HALO + dlt + tm, cols]
        cnt = (jnp.minimum(pos + w // 2, seq) - jnp.maximum(pos - w // 2, 0)).astype(F32)
        pooled = tot / cnt - hs_ref[POOL_HALO:POOL_HALO + tm, cols]
        y_parts.append(_dot(pooled.astype(MXU_DTYPE), wp_ref[g]))
    y = jnp.concatenate(y_parts, axis=1) * ps_ref[...]
    _residual_norm2(x, y, mod_ref, g2_ref, x1_ref, h2_ref)


def _pool(x2, mod, mod_index, g1, g2, wp, ps, seq):
    n, d = x2.shape
    tm = TM_DENSE
    seq_tiles = seq // tm
    hb = tm // POOL_HALO
    nh = n // POOL_HALO
    row = lambda i: (i, 0)
    const = lambda i: (0, 0)
    return pl.pallas_call(
        functools.partial(_pool_kernel, seq_tiles=seq_tiles, seq=seq),
        grid=(n // tm,),
        in_specs=[pl.BlockSpec((POOL_HALO, d), lambda i: (jnp.maximum(i * hb - 1, 0), 0)),
                  pl.BlockSpec((tm, d), row),
                  pl.BlockSpec((POOL_HALO, d), lambda i: (jnp.minimum((i + 1) * hb, nh - 1), 0)),
                  pl.BlockSpec((1, N_MOD, d), mod_index),
                  pl.BlockSpec((1, d), const), pl.BlockSpec((1, d), const),
                  pl.BlockSpec(wp.shape, lambda i: (0, 0, 0)), pl.BlockSpec((1, d), const)],
        out_specs=_mixer_outputs(n, d, tm)[0],
        out_shape=_mixer_outputs(n, d, tm)[1],
        scratch_shapes=[pltpu.VMEM((tm + 2 * POOL_HALO, d), F32)],
        compiler_params=pltpu.CompilerParams(vmem_limit_bytes=VMEM_LIMIT),
        name="pool_mixer_residual",
    )(x2, x2, x2, mod, g1, g2, wp, ps)


def _top16(s):
    t = s.shape[1]
    rows = lax.broadcasted_iota(jnp.int32, s.shape, 0).astype(F32)
    r16 = lax.broadcasted_iota(jnp.int32, (PEER_TOPK, t), 0)

    def body(r, carry):
        s, tv, ti = carry
        m = jnp.max(s, axis=0, keepdims=True)
        am = jnp.min(jnp.where(s == m, rows, float(PEER_NKEYS)), axis=0, keepdims=True)
        tv = jnp.where(r16 == r, m, tv)
        ti = jnp.where(r16 == r, am, ti)
        s = jnp.where(rows == am, -jnp.inf, s)
        return s, tv, ti

    _, tv, ti = lax.fori_loop(0, PEER_TOPK, body,
                              (s, jnp.zeros((PEER_TOPK, t), F32), jnp.zeros((PEER_TOPK, t), F32)), unroll=4)
    return tv, ti


def _combine16(tv1, ti1, tv2, ti2):
    k = PEER_TOPK
    t = tv1.shape[1]
    half = k // 2
    sv = [tv1[0:1] + tv2]
    ev = [ti1[0:1] * PEER_NKEYS + ti2]
    for a in range(1, half):
        sv.append(tv1[a:a + 1] + tv2[0:half])
        ev.append(ti1[a:a + 1] * PEER_NKEYS + ti2[0:half])
    sv.append(tv1[half:k] + tv2[0:1])
    ev.append(ti1[half:k] * PEER_NKEYS + ti2[0:1])
    cand = jnp.concatenate(sv, axis=0)
    cexp = jnp.concatenate(ev, axis=0)
    nrow = cand.shape[0]
    r = lax.broadcasted_iota(jnp.int32, (nrow, t), 0)
    mid = r - k
    hbits = half.bit_length() - 1
    flat = jnp.where(r < k, r,
                     jnp.where(r < k + (half - 1) * half, ((mid >> hbits) + 1) * k + (mid & (half - 1)),
                               (r - (k + (half - 1) * half) + half) * k)).astype(F32)
    r16 = lax.broadcasted_iota(jnp.int32, (k, t), 0)

    def body(j, carry):
        cand, bs, be = carry
        m = jnp.max(cand, axis=0, keepdims=True)
        sel = jnp.min(jnp.where(cand == m, flat, float(k * k)), axis=0, keepdims=True)
        hit = flat == sel
        e = jnp.max(jnp.where(hit, cexp, -1.0), axis=0, keepdims=True)
        bs = jnp.where(r16 == j, m, bs)
        be = jnp.where(r16 == j, e, be)
        cand = jnp.where(hit, -jnp.inf, cand)
        return cand, bs, be

    _, bs, be = lax.fori_loop(0, k, body, (cand, jnp.zeros((k, t), F32), jnp.zeros((k, t), F32)), unroll=4)
    w = jnp.exp(bs - bs[0:1])
    gate = w / jnp.sum(w, axis=0, keepdims=True)
    return gate, be.astype(jnp.int32)


def _router_kernel(h2_ref, wq_ref, sk_ref, idx_ref, gate_ref, st_ref):
    hb = h2_ref[...].astype(MXU_DTYPE)
    qt = _dot_nt(wq_ref[...], hb)
    for hp in range(PEER_HEADS * 2):
        blk = qt[hp * PEER_DHALF:(hp + 1) * PEER_DHALF].astype(MXU_DTYPE)
        st_ref[hp] = _dot(sk_ref[hp % 2], blk)
    for h in range(PEER_HEADS):
        tv1, ti1 = _top16(st_ref[2 * h])
        tv2, ti2 = _top16(st_ref[2 * h + 1])
        gate, be = _combine16(tv1, ti1, tv2, ti2)
        idx_ref[h * PEER_TOPK:(h + 1) * PEER_TOPK, :] = be
        gate_ref[h * PEER_TOPK:(h + 1) * PEER_TOPK, :] = gate


def _router(h2, wq_t, sk):
    n, d = h2.shape
    tm = TM_DENSE
    col = lambda i: (0, i)
    return pl.pallas_call(
        _router_kernel,
        grid=(n // tm,),
        in_specs=[pl.BlockSpec((tm, d), lambda i: (i, 0)),
                  pl.BlockSpec(wq_t.shape, lambda i: (0, 0)),
                  pl.BlockSpec(sk.shape, lambda i: (0, 0, 0))],
        out_specs=[pl.BlockSpec((PEER_SEL, tm), col), pl.BlockSpec((PEER_SEL, tm), col)],
        out_shape=[jax.ShapeDtypeStruct((PEER_SEL, n), jnp.int32), jax.ShapeDtypeStruct((PEER_SEL, n), F32)],
        scratch_shapes=[pltpu.VMEM((PEER_HEADS * 2, PEER_NKEYS, tm), F32)],
        compiler_params=pltpu.CompilerParams(vmem_limit_bytes=VMEM_LIMIT),
        name="peer_router",
    )(h2, wq_t, sk)


def _gather_kernel(idx_hbm, gate_ref, h2_ref, x1_ref, mod_ref, rep_ref, rept_ref, tab_hbm, out_ref,
                   idx_smem, buf, m_ref, isem, gsem, *, tb):
    i = pl.program_id(0)
    nsteps = pl.num_programs(0)
    nchunk = D_MODEL // LANES
    tile_rows = 2 * nchunk
    tok_rows = PEER_SEL * tile_rows

    def idx_copy(block, sl):
        return pltpu.make_async_copy(idx_hbm.at[pl.ds(block * tb, tb), :], idx_smem.at[sl], isem.at[sl])

    def row_copy(sl, t, s, e):
        return pltpu.make_async_copy(tab_hbm.at[e],
                                     buf.at[sl, pl.ds(t * tok_rows + s * tile_rows, tile_rows), :],
                                     gsem.at[sl * tb + t])

    def issue(sl, t, sels):
        for s in sels:
            row_copy(sl, t, s, idx_smem[sl, t, s]).start(priority=s % 2)

    def wait_token(sl, t):
        for s in range(PEER_SEL):
            row_copy(sl, t, s, 0).wait()

    @pl.when(i == 0)
    def _():
        c = idx_copy(0, 0)
        c.start()
        c.wait()

        def tok(t, carry):
            issue(0, t, range(PEER_SEL))
            return carry
        lax.fori_loop(0, tb, tok, 0)
        idx_copy(1, 1).start()

    g2 = mod_ref[0, 5:6, :]
    lane16 = lax.broadcasted_iota(jnp.int32, (nchunk, tok_rows), 1) & (tile_rows - 1)
    chunk = lax.broadcasted_iota(jnp.int32, (nchunk, tok_rows), 0)
    down_diag = (lane16 == chunk).astype(F32)
    up_diag = (lane16 == chunk + nchunk).astype(F32)
    sub = lax.broadcasted_iota(jnp.int32, (SUBLANES, LANES), 0)
    zpad = jnp.zeros((nchunk, LANES), F32)
    half = PEER_SEL // 2

    mxu_rows = 256
    nsplit = tok_rows // mxu_rows
    group_copies = SUBLANES * PEER_SEL
    mid_split = group_copies // 16
    per_split = (group_copies - 4 * mid_split) // (2 * SUBLANES * nsplit)

    def tiles(sl, t, k):
        return buf[sl, t * tok_rows + k * mxu_rows:t * tok_rows + (k + 1) * mxu_rows, :]

    def phase(sl):
        for t8 in range(tb // SUBLANES):
            r0 = sl * tb + t8 * SUBLANES
            for j in range(SUBLANES):
                wait_token(sl, t8 * SUBLANES + j)
            todo = [(t8 * SUBLANES + j, s) for j in range(SUBLANES) for s in range(PEER_SEL)]

            def take(n):
                for t, s in todo[:n]:
                    row_copy(1 - sl, t, s, idx_smem[1 - sl, t, s]).start(priority=s % 2)
                del todo[:n]

            for j in range(SUBLANES):
                t = t8 * SUBLANES + j
                xrow = h2_ref[r0 + j:r0 + j + 1, :]
                x8 = zpad
                for c in range(nchunk):
                    x8 = jnp.where(sub == c, xrow[:, c * LANES:(c + 1) * LANES], x8)
                x16 = jnp.concatenate([x8, zpad], axis=0).astype(MXU_DTYPE)
                for k in range(nsplit):
                    cols = slice(k * mxu_rows, (k + 1) * mxu_rows)
                    part = _dot_nt(x16, tiles(sl, t, k))[0:nchunk] * down_diag[:, cols]
                    m_ref[j * nchunk:(j + 1) * nchunk, cols] = part
                    take(per_split)
            m_all = m_ref[...]
            m_hi = m_all.astype(MXU_DTYPE)
            m_lo = (m_all - m_hi.astype(F32)).astype(MXU_DTYPE)
            take(mid_split)
            e2 = _dot(jnp.concatenate([m_hi, m_lo], axis=0), rept_ref[...])
            e = e2[0:SUBLANES * nchunk] + e2[SUBLANES * nchunk:]
            take(mid_split)
            d = jnp.zeros((SUBLANES, PEER_SEL), F32)
            for j in range(SUBLANES):
                d = jnp.where(sub == j, jnp.sum(e[j * nchunk:(j + 1) * nchunk], axis=0, keepdims=True), d)
            a = gate_ref[r0:r0 + SUBLANES, :] * jax.nn.gelu(d)
            take(mid_split)
            a_exp = _dot(a.astype(MXU_DTYPE), rep_ref[...])
            take(mid_split)
            for j in range(SUBLANES):
                t = t8 * SUBLANES + j
                v = jnp.broadcast_to(a_exp[j:j + 1], (nchunk, tok_rows)) * up_diag
                v16 = jnp.concatenate([v, jnp.zeros_like(v)], axis=0).astype(MXU_DTYPE)
                y = jnp.zeros((tile_rows, LANES), F32)
                for k in range(nsplit):
                    y = y + _dot(v16[:, k * mxu_rows:(k + 1) * mxu_rows], tiles(sl, t, k))
                    take(per_split)
                for c in range(nchunk):
                    cols = slice(c * LANES, (c + 1) * LANES)
                    out_ref[r0 + j:r0 + j + 1, cols] = x1_ref[r0 + j:r0 + j + 1, cols] + g2[:, cols] * y[c:c + 1, :]
            take(len(todo))

    more = i + 1 < nsteps
    idx_copy(2 * i + 1, 1).wait()

    @pl.when(more)
    def _():
        idx_copy(2 * i + 2, 0).start()

    phase(0)

    @pl.when(more)
    def _():
        idx_copy(2 * i + 2, 0).wait()
        idx_copy(2 * i + 3, 1).start()

    phase(1)

    @pl.when(jnp.logical_not(more))
    def _():
        for t in range(tb):
            wait_token(0, t)


def _gather(idx_tm, gate_tm, h2, x1, mod, mod_index, table):
    n, d = x1.shape
    nchunk = d // LANES
    tb = TB_GATHER
    step = 2 * tb
    tile_rows = 2 * nchunk
    tok_rows = PEER_SEL * tile_rows
    rep = (jnp.arange(tok_rows)[None, :] // tile_rows == jnp.arange(PEER_SEL)[:, None]).astype(MXU_DTYPE)
    row = lambda i: (i, 0)
    const = lambda i: (0, 0)
    return pl.pallas_call(
        functools.partial(_gather_kernel, tb=tb),
        grid=(n // step,),
        in_specs=[pl.BlockSpec(memory_space=pl.ANY),
                  pl.BlockSpec((step, PEER_SEL), row),
                  pl.BlockSpec((step, d), row), pl.BlockSpec((step, d), row),
                  pl.BlockSpec((1, N_MOD, d), mod_index),
                  pl.BlockSpec((PEER_SEL, tok_rows), const),
                  pl.BlockSpec((tok_rows, PEER_SEL), const),
                  pl.BlockSpec(memory_space=pl.ANY)],
        out_specs=pl.BlockSpec((step, d), row),
        out_shape=jax.ShapeDtypeStruct((n, d), F32),
        scratch_shapes=[pltpu.SMEM((2, tb, PEER_SEL), jnp.int32),
                        pltpu.VMEM((2, tb * tok_rows, LANES), MXU_DTYPE),
                        pltpu.VMEM((SUBLANES * nchunk, tok_rows), F32),
                        pltpu.SemaphoreType.DMA((2,)),
                        pltpu.SemaphoreType.DMA((2 * tb,))],
        compiler_params=pltpu.CompilerParams(dimension_semantics=("arbitrary",), vmem_limit_bytes=VMEM_LIMIT),
        name="peer_gather",
    )(idx_tm, gate_tm, h2, x1, mod, rep, rep.T, table)


def _peer(h2, x1, mod, mod_index, wq_t, sk, table):
    idx, gate = _router(h2, wq_t, sk)
    return _gather(idx.T, gate.T, h2, x1, mod, mod_index, table)


def _final_norm_kernel(x_ref, g_ref, o_ref):
    o_ref[...] = _rms(x_ref[...]) * g_ref[...]


def _final_norm(x2, g):
    n, d = x2.shape
    tm = 2 * TM_DENSE
    return pl.pallas_call(
        _final_norm_kernel,
        grid=(n // tm,),
        in_specs=[pl.BlockSpec((tm, d), lambda i: (i, 0)), pl.BlockSpec((1, d), lambda i: (0, 0))],
        out_specs=pl.BlockSpec((tm, d), lambda i: (i, 0)),
        out_shape=jax.ShapeDtypeStruct((n, d), F32),
        name="final_rmsnorm",
    )(x2, g)


def _pack_table(down, up):
    e, d = down.shape
    chunks = (e, d // LANES, LANES)
    return jnp.concatenate([down.reshape(chunks), up.reshape(chunks)], axis=1).astype(MXU_DTYPE)


def _rope_tables(seq):
    quarter = HEAD_DIM // 4
    inv = jnp.power(ROPE_BASE, -jnp.arange(quarter, dtype=F32) / quarter)
    t = jnp.arange(seq)
    row = (t // GRID_W).astype(F32)
    col = (t % GRID_W).astype(F32)
    ang_r = row[:, None] * inv[None, :]
    ang_c = col[:, None] * inv[None, :]
    cos = jnp.concatenate([jnp.cos(ang_r)] * 2 + [jnp.cos(ang_c)] * 2, axis=1)
    sin = jnp.concatenate([-jnp.sin(ang_r), jnp.sin(ang_r), -jnp.sin(ang_c), jnp.sin(ang_c)], axis=1)
    reps = LANES // HEAD_DIM
    return jnp.tile(cos, (1, reps)), jnp.tile(sin, (1, reps))


def _inproj_weights(w_in):
    d = w_in.shape[0]
    slots = []
    for h in range(ATTN_HEADS):
        g = h // GQA_GROUP
        blk = w_in[:, h * HEAD_DIM:(h + 1) * HEAD_DIM]
        z = jnp.zeros((d, HEAD_DIM), w_in.dtype)
        slots.append(jnp.concatenate([blk, z] if g == 0 else [z, blk], axis=1))
    return jnp.concatenate(slots + [w_in[:, Q_W:]], axis=1).astype(MXU_DTYPE)


def _outproj_weights(w_out):
    d = w_out.shape[1]
    slots = []
    for h in range(ATTN_HEADS):
        g = h // GQA_GROUP
        blk = w_out[h * HEAD_DIM:(h + 1) * HEAD_DIM]
        z = jnp.zeros((HEAD_DIM, d), w_out.dtype)
        slots.append(jnp.concatenate([blk, z] if g == 0 else [z, blk], axis=0))
    return jnp.concatenate(slots, axis=0).astype(MXU_DTYPE), w_out[Q_W:].astype(MXU_DTYPE)


def kernel(x, c, ctx, c_ctx, mod_w, mod_b, norm1_g, norm2_g, attn_in_w, attn_out_w, attn_sink, gmlp_w_s, gmlp_b_s, pool_w, pool_scale, peer_w_q, peer_sub_keys, peer_down, peer_up, final_g):
    batch, seq, d = x.shape
    n = batch * seq
    n_ctx = ctx.shape[1]
    depth = mod_w.shape[0]
    assert depth == 2 and d == D_MODEL
    assert seq % TM_DENSE == 0 and n_ctx % TM_DENSE == 0 and LANES % (2 * TB_GATHER) == 0
    assert n % (2 * TM_DENSE) == 0

    mod_rows = -(-(batch + 1) // SUBLANES) * SUBLANES
    cs = jnp.zeros((mod_rows, d), F32).at[:batch].set(c).at[batch].set(c_ctx)
    mods = _modulation(cs, mod_w, mod_b)
    seq_tiles = seq // TM_DENSE
    main_mod = lambda i: (i // seq_tiles, 0, 0)
    ctx_mod = lambda i: (batch, 0, 0)
    gather_mod = lambda i: (i // (seq // (2 * TB_GATHER)), 0, 0)

    x2 = x.reshape(n, d)
    row1 = lambda v: v.reshape(1, -1)

    w_ext = _inproj_weights(attn_in_w[0])
    cos, sin = _rope_tables(seq)
    m64 = jnp.kron(jnp.eye(GMLP_GROUPS, dtype=F32), jnp.full((GMLP_DIM, GMLP_DIM), 1.0 / GMLP_DIM, F32)
                   ).astype(MXU_DTYPE)
    q, k, v, u, gn = _inproj(x2, mods[0], main_mod, row1(norm1_g[0]), w_ext, cos, sin, m64, seq_tiles)
    ones = jnp.ones((n_ctx, LANES), F32)
    _, kx, vx, _, _ = _inproj(ctx.reshape(batch * n_ctx, d), mods[0], ctx_mod, row1(norm1_g[0]), w_ext,
                              ones, jnp.zeros_like(ones), m64, n_ctx // TM_DENSE)
    ws = gmlp_w_s[0].astype(MXU_DTYPE)
    bt = gmlp_b_s[0]
    bs = jnp.stack([jnp.concatenate([jnp.broadcast_to(bt[2 * j][:, None], (CHUNK, GMLP_DIM)),
                                     jnp.broadcast_to(bt[2 * j + 1][:, None], (CHUNK, GMLP_DIM))], axis=1)
                    for j in range(GMLP_GROUPS // 2)])
    attn, mix = _attn_gmlp(attn_sink[0], q, k, v, kx, vx, u, gn, ws, bs, batch, seq)
    wa, wm = _outproj_weights(attn_out_w[0])
    x1, h2 = _outproj(x2, attn, mix, wa, wm, mods[0], main_mod, row1(norm2_g[0]))
    xl = _peer(h2, x1, mods[0], gather_mod, peer_w_q[0].T.astype(MXU_DTYPE),
               peer_sub_keys[0].astype(MXU_DTYPE), _pack_table(peer_down[0], peer_up[0]))

    x1, h2 = _pool(xl, mods[1], main_mod, row1(norm1_g[1]), row1(norm2_g[1]), pool_w[0].astype(MXU_DTYPE),
                   row1(pool_scale[0]), seq)
    xl = _peer(h2, x1, mods[1], gather_mod, peer_w_q[1].T.astype(MXU_DTYPE),
               peer_sub_keys[1].astype(MXU_DTYPE), _pack_table(peer_down[1], peer_up[1]))
    return _final_norm(xl, row1(final_g)).reshape(batch, seq, d)
```

```python
import functools

import jax
import jax.numpy as jnp
from jax import lax
from jax.experimental import pallas as pl
from jax.experimental.pallas import tpu as pltpu

D_MODEL = 1024
GRID_W = 64
HEAD_DIM = 64
ATTN_HEADS = 8
KV_HEADS = 2
GQA_GROUP = ATTN_HEADS // KV_HEADS
WINDOW = 128
ATTN_BLOCK = 128
ROPE_BASE = 10000.0
GMLP_GROUPS = 8
GMLP_DIM = 64
CHUNK = 128
Q_W = ATTN_HEADS * HEAD_DIM
KV_W = KV_HEADS * HEAD_DIM
A_W = GMLP_GROUPS * GMLP_DIM
POOL_WINDOWS = (2, 4, 8, 16)
POOL_GROUPS = 4
POOL_DIM = D_MODEL // POOL_GROUPS
POOL_HALO = 8
PEER_HEADS = 8
PEER_NKEYS = 128
PEER_TOPK = 16
PEER_DKEY = 256
PEER_DHALF = PEER_DKEY // 2
PEER_SEL = PEER_HEADS * PEER_TOPK
N_MOD = 6
EPS = 1e-6

LANES = 128
SUBLANES = 8
QPAD_W = ATTN_HEADS * LANES
MXU_DTYPE = jnp.bfloat16
NEG = -1e30
F32 = jnp.float32

TM_DENSE = 256
TB_GATHER = 16
VMEM_LIMIT = 48 * 1024 * 1024


def _rms(x):
    return x * lax.rsqrt(jnp.mean(x * x, axis=-1, keepdims=True) + EPS)


def _dot(a, b):
    return jnp.dot(a, b, preferred_element_type=F32)


def _dot_nt(a, b):
    return lax.dot_general(a, b, (((1,), (1,)), ((), ())), preferred_element_type=F32)


def _mod_kernel(c_ref, w_ref, b_ref, o_ref):
    c = c_ref[...]
    s = c * jax.nn.sigmoid(c)
    o_ref[0] = jnp.dot(s, w_ref[0], preferred_element_type=F32, precision=lax.Precision.HIGHEST) + b_ref[0]


def _modulation(cs, mod_w, mod_b):
    depth, d, n6 = mod_w.shape
    r = cs.shape[0]
    tn = 512
    out = pl.pallas_call(
        _mod_kernel,
        grid=(depth, n6 // tn),
        in_specs=[pl.BlockSpec((r, d), lambda l, j: (0, 0)),
                  pl.BlockSpec((1, d, tn), lambda l, j: (l, 0, j)),
                  pl.BlockSpec((1, 1, tn), lambda l, j: (l, 0, j))],
        out_specs=pl.BlockSpec((1, r, tn), lambda l, j: (l, 0, j)),
        out_shape=jax.ShapeDtypeStruct((depth, r, n6), F32),
        name="adaln_modulation",
    )(cs, mod_w, mod_b.reshape(depth, 1, n6))
    return out.reshape(depth, r, N_MOD, d)


def _rope(t, cos, sin, first_half):
    partner = jnp.where(first_half, pltpu.roll(t, LANES - 16, 1), pltpu.roll(t, 16, 1))
    return t * cos + partner * sin


def _inproj_kernel(x_ref, mod_ref, g_ref, w_ref, cos_ref, sin_ref, m64_ref,
                   q_ref, k_ref, v_ref, u_ref, vn_ref):
    x = x_ref[...]
    sh = mod_ref[0, 0:1, :]
    sc = mod_ref[0, 1:2, :]
    h = _rms(x) * g_ref[...] * (1.0 + sc) + sh
    p = _dot(h.astype(MXU_DTYPE), w_ref[...])
    cos = cos_ref[...]
    sin = sin_ref[...]
    lane = lax.broadcasted_iota(jnp.int32, cos.shape, 1)
    first_half = (lane & 31) < 16
    scale = HEAD_DIM ** -0.5
    for s in range(ATTN_HEADS):
        t = _rope(p[:, s * LANES:(s + 1) * LANES], cos, sin, first_half)
        q_ref[:, s * LANES:(s + 1) * LANES] = (t * scale).astype(q_ref.dtype)
    o = QPAD_W
    k_ref[...] = _rope(p[:, o:o + KV_W], cos, sin, first_half).astype(k_ref.dtype)
    v_ref[...] = p[:, o + KV_W:o + 2 * KV_W].astype(v_ref.dtype)
    o += 2 * KV_W
    u_ref[...] = jax.nn.gelu(p[:, o:o + A_W]).astype(u_ref.dtype)
    vg = jax.nn.gelu(p[:, o + A_W:o + 2 * A_W])
    m64 = m64_ref[...]
    mu = _dot(vg.astype(MXU_DTYPE), m64)
    cen = vg - mu
    var = _dot((cen * cen).astype(MXU_DTYPE), m64)
    vn_ref[...] = (cen * lax.rsqrt(var + EPS)).astype(vn_ref.dtype)


def _inproj(x2, mod, mod_index, g, w_ext, cos, sin, m64, seq_tiles):
    n, d = x2.shape
    tm = TM_DENSE
    wcols = w_ext.shape[1]
    row = lambda i: (i, 0)
    const = lambda i: (0, 0)
    outs = [jax.ShapeDtypeStruct((n, QPAD_W), MXU_DTYPE),
            jax.ShapeDtypeStruct((n, KV_W), MXU_DTYPE),
            jax.ShapeDtypeStruct((n, KV_W), MXU_DTYPE),
            jax.ShapeDtypeStruct((n, A_W), MXU_DTYPE),
            jax.ShapeDtypeStruct((n, A_W), MXU_DTYPE)]
    return pl.pallas_call(
        _inproj_kernel,
        grid=(n // tm,),
        in_specs=[pl.BlockSpec((tm, d), row),
                  pl.BlockSpec((1, N_MOD, d), mod_index),
                  pl.BlockSpec((1, d), const),
                  pl.BlockSpec((d, wcols), const),
                  pl.BlockSpec((tm, LANES), lambda i: (i % seq_tiles, 0)),
                  pl.BlockSpec((tm, LANES), lambda i: (i % seq_tiles, 0)),
                  pl.BlockSpec((A_W, A_W), const)],
        out_specs=[pl.BlockSpec((tm, QPAD_W), row), pl.BlockSpec((tm, KV_W), row), pl.BlockSpec((tm, KV_W), row),
                   pl.BlockSpec((tm, A_W), row), pl.BlockSpec((tm, A_W), row)],
        out_shape=outs,
        compiler_params=pltpu.CompilerParams(vmem_limit_bytes=VMEM_LIMIT),
        name="inproj_rope",
    )(x2, mod, g, w_ext, cos, sin, m64)


def _attn_gmlp_kernel(sink_ref, q_ref, kp_ref, kc_ref, kn_ref, vp_ref, vc_ref, vn_ref, kx_ref, vx_ref,
                      u_ref, g_ref, ws_ref, bs_ref, attn_ref, mix_ref):
    n = pl.program_id(1)
    nb = pl.num_programs(1)
    rows = GQA_GROUP * ATTN_BLOCK
    ri = lax.broadcasted_iota(jnp.int32, (rows, ATTN_BLOCK), 0) & (ATTN_BLOCK - 1)
    ci = lax.broadcasted_iota(jnp.int32, (rows, ATTN_BLOCK), 1)
    mask_prev = jnp.logical_and(ci >= ri, n > 0)
    mask_next = jnp.logical_and(ci <= ri, n < nb - 1)
    rblk = lax.broadcasted_iota(jnp.int32, (rows, 1), 0) >> (ATTN_BLOCK.bit_length() - 1)
    kp, kc, kn, kx = kp_ref[...], kc_ref[...], kn_ref[...], kx_ref[...]
    vp, vc, vn, vx = vp_ref[...], vc_ref[...], vn_ref[...], vx_ref[...]
    for g in range(KV_HEADS):
        h0 = g * GQA_GROUP
        qg = jnp.concatenate([q_ref[:, (h0 + j) * LANES:(h0 + j + 1) * LANES] for j in range(GQA_GROUP)], axis=0)
        sink = jnp.full((rows, 1), sink_ref[h0], F32)
        for j in range(1, GQA_GROUP):
            sink = jnp.where(rblk == j, sink_ref[h0 + j], sink)
        s_p = jnp.where(mask_prev, _dot_nt(qg, kp), NEG)
        s_c = _dot_nt(qg, kc)
        s_n = jnp.where(mask_next, _dot_nt(qg, kn), NEG)
        s_x = _dot_nt(qg, kx)
        m = jnp.maximum(jnp.maximum(jnp.max(s_p, axis=1, keepdims=True), jnp.max(s_c, axis=1, keepdims=True)),
                        jnp.maximum(jnp.max(s_n, axis=1, keepdims=True), jnp.max(s_x, axis=1, keepdims=True)))
        m = jnp.maximum(m, sink)
        p_p, p_c, p_n, p_x = jnp.exp(s_p - m), jnp.exp(s_c - m), jnp.exp(s_n - m), jnp.exp(s_x - m)
        den = (jnp.sum(p_p, axis=1, keepdims=True) + jnp.sum(p_c, axis=1, keepdims=True)
               + jnp.sum(p_n, axis=1, keepdims=True) + jnp.sum(p_x, axis=1, keepdims=True) + jnp.exp(sink - m))
        o = (_dot(p_p.astype(MXU_DTYPE), vp) + _dot(p_c.astype(MXU_DTYPE), vc)
             + _dot(p_n.astype(MXU_DTYPE), vn) + _dot(p_x.astype(MXU_DTYPE), vx))
        o = o / den
        for j in range(GQA_GROUP):
            attn_ref[:, (h0 + j) * LANES:(h0 + j + 1) * LANES] = (
                o[j * ATTN_BLOCK:(j + 1) * ATTN_BLOCK].astype(attn_ref.dtype))
    lane = lax.broadcasted_iota(jnp.int32, (CHUNK, LANES), 1)
    for j in range(GMLP_GROUPS // 2):
        gp = g_ref[:, j * LANES:(j + 1) * LANES]
        mixed = jnp.where(lane < GMLP_DIM, _dot(ws_ref[2 * j], gp), _dot(ws_ref[2 * j + 1], gp)) + bs_ref[j]
        mix_ref[:, j * LANES:(j + 1) * LANES] = (u_ref[:, j * LANES:(j + 1) * LANES].astype(F32) * mixed
                                                 ).astype(mix_ref.dtype)


def _attn_gmlp(sink, q, k, v, kx, vx, u, gn, ws, bs, batch, seq):
    n = q.shape[0]
    nb = seq // ATTN_BLOCK
    cx = kx.shape[0] // batch
    blk = ATTN_BLOCK
    cur = lambda b, i: (b * nb + i, 0)
    prev = lambda b, i: (b * nb + jnp.maximum(i - 1, 0), 0)
    nxt = lambda b, i: (b * nb + jnp.minimum(i + 1, nb - 1), 0)
    ctx = lambda b, i: (b, 0)
    return pl.pallas_call(
        _attn_gmlp_kernel,
        grid=(batch, nb),
        in_specs=[pl.BlockSpec(memory_space=pltpu.SMEM),
                  pl.BlockSpec((blk, QPAD_W), cur),
                  pl.BlockSpec((blk, KV_W), prev), pl.BlockSpec((blk, KV_W), cur), pl.BlockSpec((blk, KV_W), nxt),
                  pl.BlockSpec((blk, KV_W), prev), pl.BlockSpec((blk, KV_W), cur), pl.BlockSpec((blk, KV_W), nxt),
                  pl.BlockSpec((cx, KV_W), ctx), pl.BlockSpec((cx, KV_W), ctx),
                  pl.BlockSpec((blk, A_W), cur), pl.BlockSpec((blk, A_W), cur),
                  pl.BlockSpec(ws.shape, lambda b, i: (0, 0, 0)),
                  pl.BlockSpec(bs.shape, lambda b, i: (0, 0, 0))],
        out_specs=[pl.BlockSpec((blk, QPAD_W), cur), pl.BlockSpec((blk, A_W), cur)],
        out_shape=[jax.ShapeDtypeStruct((n, QPAD_W), MXU_DTYPE), jax.ShapeDtypeStruct((n, A_W), MXU_DTYPE)],
        compiler_params=pltpu.CompilerParams(vmem_limit_bytes=VMEM_LIMIT),
        name="window_attn_gmlp",
    )(sink, q, k, k, k, v, v, v, kx, vx, u, gn, ws, bs)


def _residual_norm2(x, y, mod_ref, g2_ref, x1_ref, h2_ref):
    g1 = mod_ref[0, 2:3, :]
    sh2 = mod_ref[0, 3:4, :]
    sc2 = mod_ref[0, 4:5, :]
    x1 = x + g1 * y
    x1_ref[...] = x1
    h2_ref[...] = _rms(x1) * g2_ref[...] * (1.0 + sc2) + sh2


def _outproj_kernel(x_ref, a_ref, m_ref, wa_ref, wm_ref, mod_ref, g2_ref, x1_ref, h2_ref):
    y = _dot(a_ref[...], wa_ref[...]) + _dot(m_ref[...], wm_ref[...])
    _residual_norm2(x_ref[...], y, mod_ref, g2_ref, x1_ref, h2_ref)


def _mixer_outputs(n, d, tm):
    specs = [pl.BlockSpec((tm, d), lambda i: (i, 0)), pl.BlockSpec((tm, d), lambda i: (i, 0))]
    shapes = [jax.ShapeDtypeStruct((n, d), F32), jax.ShapeDtypeStruct((n, d), F32)]
    return specs, shapes


def _outproj(x2, attn, mix, wa, wm, mod, mod_index, g2):
    n, d = x2.shape
    tm = TM_DENSE
    row = lambda i: (i, 0)
    const = lambda i: (0, 0)
    out_specs, out_shape = _mixer_outputs(n, d, tm)
    return pl.pallas_call(
        _outproj_kernel,
        grid=(n // tm,),
        in_specs=[pl.BlockSpec((tm, d), row), pl.BlockSpec((tm, QPAD_W), row), pl.BlockSpec((tm, A_W), row),
                  pl.BlockSpec(wa.shape, const), pl.BlockSpec(wm.shape, const),
                  pl.BlockSpec((1, N_MOD, d), mod_index), pl.BlockSpec((1, d), const)],
        out_specs=out_specs,
        out_shape=out_shape,
        compiler_params=pltpu.CompilerParams(vmem_limit_bytes=VMEM_LIMIT),
        name="outproj_residual",
    )(x2, attn, mix, wa, wm, mod, g2)


def _pool_kernel(xp_ref, xc_ref, xn_ref, mod_ref, g1_ref, g2_ref, wp_ref, ps_ref, x1_ref, h2_ref, hs_ref,
                 *, seq_tiles, seq):
    i = pl.program_id(0)
    tm = xc_ref.shape[0]
    sh = mod_ref[0, 0:1, :]
    sc = mod_ref[0, 1:2, :]
    g1 = g1_ref[...]

    def norm(x):
        return _rms(x) * g1 * (1.0 + sc) + sh

    it = i % seq_tiles
    x = xc_ref[...]
    hs_ref[0:POOL_HALO, :] = jnp.where(it > 0, norm(xp_ref[...]), 0.0)
    hs_ref[POOL_HALO:POOL_HALO + tm, :] = norm(x)
    hs_ref[POOL_HALO + tm:, :] = jnp.where(it < seq_tiles - 1, norm(xn_ref[...]), 0.0)
    pos = it * tm + lax.broadcasted_iota(jnp.int32, (tm, 1), 0)
    y_parts = []
    for g, w in enumerate(POOL_WINDOWS):
        cols = slice(g * POOL_DIM, (g + 1) * POOL_DIM)
        tot = hs_ref[POOL_HALO - w // 2:POOL_HALO - w // 2 + tm, cols]
        for dlt in range(-w // 2 + 1, w // 2):
            tot = tot + hs_ref[POOL_HALO + dlt:POOL_HALO + dlt + tm, cols]
        cnt = (jnp.minimum(pos + w // 2, seq) - jnp.maximum(pos - w // 2, 0)).astype(F32)
        pooled = tot / cnt - hs_ref[POOL_HALO:POOL_HALO + tm, cols]
        y_parts.append(_dot(pooled.astype(MXU_DTYPE), wp_ref[g]))
    y = jnp.concatenate(y_parts, axis=1) * ps_ref[...]
    _residual_norm2(x, y, mod_ref, g2_ref, x1_ref, h2_ref)


def _pool(x2, mod, mod_index, g1, g2, wp, ps, seq):
    n, d = x2.shape
    tm = TM_DENSE
    seq_tiles = seq // tm
    hb = tm // POOL_HALO
    nh = n // POOL_HALO
    row = lambda i: (i, 0)
    const = lambda i: (0, 0)
    return pl.pallas_call(
        functools.partial(_pool_kernel, seq_tiles=seq_tiles, seq=seq),
        grid=(n // tm,),
        in_specs=[pl.BlockSpec((POOL_HALO, d), lambda i: (jnp.maximum(i * hb - 1, 0), 0)),
                  pl.BlockSpec((tm, d), row),
                  pl.BlockSpec((POOL_HALO, d), lambda i: (jnp.minimum((i + 1) * hb, nh - 1), 0)),
                  pl.BlockSpec((1, N_MOD, d), mod_index),
                  pl.BlockSpec((1, d), const), pl.BlockSpec((1, d), const),
                  pl.BlockSpec(wp.shape, lambda i: (0, 0, 0)), pl.BlockSpec((1, d), const)],
        out_specs=_mixer_outputs(n, d, tm)[0],
        out_shape=_mixer_outputs(n, d, tm)[1],
        scratch_shapes=[pltpu.VMEM((tm + 2 * POOL_HALO, d), F32)],
        compiler_params=pltpu.CompilerParams(vmem_limit_bytes=VMEM_LIMIT),
        name="pool_mixer_residual",
    )(x2, x2, x2, mod, g1, g2, wp, ps)


def _top16(s):
    t = s.shape[1]
    rows = lax.broadcasted_iota(jnp.int32, s.shape, 0).astype(F32)
    r16 = lax.broadcasted_iota(jnp.int32, (PEER_TOPK, t), 0)

    def body(r, carry):
        s, tv, ti = carry
        m = jnp.max(s, axis=0, keepdims=True)
        am = jnp.min(jnp.where(s == m, rows, float(PEER_NKEYS)), axis=0, keepdims=True)
        tv = jnp.where(r16 == r, m, tv)
        ti = jnp.where(r16 == r, am, ti)
        s = jnp.where(rows == am, -jnp.inf, s)
        return s, tv, ti

    _, tv, ti = lax.fori_loop(0, PEER_TOPK, body,
                              (s, jnp.zeros((PEER_TOPK, t), F32), jnp.zeros((PEER_TOPK, t), F32)), unroll=4)
    return tv, ti


def _combine16(tv1, ti1, tv2, ti2):
    k = PEER_TOPK
    t = tv1.shape[1]
    half = k // 2
    sv = [tv1[0:1] + tv2]
    ev = [ti1[0:1] * PEER_NKEYS + ti2]
    for a in range(1, half):
        sv.append(tv1[a:a + 1] + tv2[0:half])
        ev.append(ti1[a:a + 1] * PEER_NKEYS + ti2[0:half])
    sv.append(tv1[half:k] + tv2[0:1])
    ev.append(ti1[half:k] * PEER_NKEYS + ti2[0:1])
    cand = jnp.concatenate(sv, axis=0)
    cexp = jnp.concatenate(ev, axis=0)
    nrow = cand.shape[0]
    r = lax.broadcasted_iota(jnp.int32, (nrow, t), 0)
    mid = r - k
    hbits = half.bit_length() - 1
    flat = jnp.where(r < k, r,
                     jnp.where(r < k + (half - 1) * half, ((mid >> hbits) + 1) * k + (mid & (half - 1)),
                               (r - (k + (half - 1) * half) + half) * k)).astype(F32)
    r16 = lax.broadcasted_iota(jnp.int32, (k, t), 0)

    def body(j, carry):
        cand, bs, be = carry
        m = jnp.max(cand, axis=0, keepdims=True)
        sel = jnp.min(jnp.where(cand == m, flat, float(k * k)), axis=0, keepdims=True)
        hit = flat == sel
        e = jnp.max(jnp.where(hit, cexp, -1.0), axis=0, keepdims=True)
        bs = jnp.where(r16 == j, m, bs)
        be = jnp.where(r16 == j, e, be)
        cand = jnp.where(hit, -jnp.inf, cand)
        return cand, bs, be

    _, bs, be = lax.fori_loop(0, k, body, (cand, jnp.zeros((k, t), F32), jnp.zeros((k, t), F32)), unroll=4)
    w = jnp.exp(bs - bs[0:1])
    gate = w / jnp.sum(w, axis=0, keepdims=True)
    return gate, be.astype(jnp.int32)


def _router_kernel(h2_ref, wq_ref, sk_ref, idx_ref, gate_ref, st_ref):
    hb = h2_ref[...].astype(MXU_DTYPE)
    qt = _dot_nt(wq_ref[...], hb)
    for hp in range(PEER_HEADS * 2):
        blk = qt[hp * PEER_DHALF:(hp + 1) * PEER_DHALF].astype(MXU_DTYPE)
        st_ref[hp] = _dot(sk_ref[hp % 2], blk)
    for h in range(PEER_HEADS):
        tv1, ti1 = _top16(st_ref[2 * h])
        tv2, ti2 = _top16(st_ref[2 * h + 1])
        gate, be = _combine16(tv1, ti1, tv2, ti2)
        idx_ref[h * PEER_TOPK:(h + 1) * PEER_TOPK, :] = be
        gate_ref[h * PEER_TOPK:(h + 1) * PEER_TOPK, :] = gate


def _router(h2, wq_t, sk):
    n, d = h2.shape
    tm = TM_DENSE
    col = lambda i: (0, i)
    return pl.pallas_call(
        _router_kernel,
        grid=(n // tm,),
        in_specs=[pl.BlockSpec((tm, d), lambda i: (i, 0)),
                  pl.BlockSpec(wq_t.shape, lambda i: (0, 0)),
                  pl.BlockSpec(sk.shape, lambda i: (0, 0, 0))],
        out_specs=[pl.BlockSpec((PEER_SEL, tm), col), pl.BlockSpec((PEER_SEL, tm), col)],
        out_shape=[jax.ShapeDtypeStruct((PEER_SEL, n), jnp.int32), jax.ShapeDtypeStruct((PEER_SEL, n), F32)],
        scratch_shapes=[pltpu.VMEM((PEER_HEADS * 2, PEER_NKEYS, tm), F32)],
        compiler_params=pltpu.CompilerParams(vmem_limit_bytes=VMEM_LIMIT),
        name="peer_router",
    )(h2, wq_t, sk)


def _gather_kernel(idx_hbm, gate_ref, h2_ref, x1_ref, mod_ref, rep_ref, rept_ref, tab_hbm, out_ref,
                   idx_smem, buf, m_ref, isem, gsem, *, tb):
    i = pl.program_id(0)
    nsteps = pl.num_programs(0)
    nchunk = D_MODEL // LANES
    tile_rows = 2 * nchunk
    tok_rows = PEER_SEL * tile_rows

    def idx_copy(block, sl):
        return pltpu.make_async_copy(idx_hbm.at[pl.ds(block * tb, tb), :], idx_smem.at[sl], isem.at[sl])

    def row_copy(sl, t, s, e):
        return pltpu.make_async_copy(tab_hbm.at[e],
                                     buf.at[sl, pl.ds(t * tok_rows + s * tile_rows, tile_rows), :],
                                     gsem.at[sl * tb + t])

    def issue(sl, t, sels):
        for s in sels:
            row_copy(sl, t, s, idx_smem[sl, t, s]).start(priority=s % 2)

    def wait_token(sl, t):
        for s in range(PEER_SEL):
            row_copy(sl, t, s, 0).wait()

    @pl.when(i == 0)
    def _():
        c = idx_copy(0, 0)
        c.start()
        c.wait()

        def tok(t, carry):
            issue(0, t, range(PEER_SEL))
            return carry
        lax.fori_loop(0, tb, tok, 0)
        idx_copy(1, 1).start()

    g2 = mod_ref[0, 5:6, :]
    lane16 = lax.broadcasted_iota(jnp.int32, (nchunk, tok_rows), 1) & (tile_rows - 1)
    chunk = lax.broadcasted_iota(jnp.int32, (nchunk, tok_rows), 0)
    down_diag = (lane16 == chunk).astype(F32)
    up_diag = (lane16 == chunk + nchunk).astype(F32)
    sub = lax.broadcasted_iota(jnp.int32, (SUBLANES, LANES), 0)
    zpad = jnp.zeros((nchunk, LANES), F32)
    half = PEER_SEL // 2

    mxu_rows = 256
    nsplit = tok_rows // mxu_rows
    group_copies = SUBLANES * PEER_SEL
    mid_split = group_copies // 16
    per_split = (group_copies - 4 * mid_split) // (2 * SUBLANES * nsplit)

    def tiles(sl, t, k):
        return buf[sl, t * tok_rows + k * mxu_rows:t * tok_rows + (k + 1) * mxu_rows, :]

    def phase(sl):
        for t8 in range(tb // SUBLANES):
            r0 = sl * tb + t8 * SUBLANES
            for j in range(SUBLANES):
                wait_token(sl, t8 * SUBLANES + j)
            todo = [(t8 * SUBLANES + j, s) for j in range(SUBLANES) for s in range(PEER_SEL)]

            def take(n):
                for t, s in todo[:n]:
                    row_copy(1 - sl, t, s, idx_smem[1 - sl, t, s]).start(priority=s % 2)
                del todo[:n]

            for j in range(SUBLANES):
                t = t8 * SUBLANES + j
                xrow = h2_ref[r0 + j:r0 + j + 1, :]
                x8 = zpad
                for c in range(nchunk):
                    x8 = jnp.where(sub == c, xrow[:, c * LANES:(c + 1) * LANES], x8)
                x16 = jnp.concatenate([x8, zpad], axis=0).astype(MXU_DTYPE)
                for k in range(nsplit):
                    cols = slice(k * mxu_rows, (k + 1) * mxu_rows)
                    part = _dot_nt(x16, tiles(sl, t, k))[0:nchunk] * down_diag[:, cols]
                    m_ref[j:j + 1, cols] = jnp.sum(part, axis=0, keepdims=True)
                    take(per_split)
            m_all = m_ref[...]
            m_hi = m_all.astype(MXU_DTYPE)
            m_lo = (m_all - m_hi.astype(F32)).astype(MXU_DTYPE)
            take(mid_split)
            e2 = _dot(jnp.concatenate([m_hi, m_lo], axis=0), rept_ref[...])
            d = e2[0:SUBLANES] + e2[SUBLANES:]
            take(mid_split)
            a = gate_ref[r0:r0 + SUBLANES, :] * jax.nn.gelu(d)
            take(mid_split)
            a_exp = _dot(a.astype(MXU_DTYPE), rep_ref[...])
            take(mid_split)
            for j in range(SUBLANES):
                t = t8 * SUBLANES + j
                v = jnp.broadcast_to(a_exp[j:j + 1], (nchunk, tok_rows)) * up_diag
                v16 = jnp.concatenate([v, jnp.zeros_like(v)], axis=0).astype(MXU_DTYPE)
                y = jnp.zeros((tile_rows, LANES), F32)
                for k in range(nsplit):
                    y = y + _dot(v16[:, k * mxu_rows:(k + 1) * mxu_rows], tiles(sl, t, k))
                    take(per_split)
                for c in range(nchunk):
                    cols = slice(c * LANES, (c + 1) * LANES)
                    out_ref[r0 + j:r0 + j + 1, cols] = x1_ref[r0 + j:r0 + j + 1, cols] + g2[:, cols] * y[c:c + 1, :]
            take(len(todo))

    more = i + 1 < nsteps
    idx_copy(2 * i + 1, 1).wait()

    @pl.when(more)
    def _():
        idx_copy(2 * i + 2, 0).start()

    phase(0)

    @pl.when(more)
    def _():
        idx_copy(2 * i + 2, 0).wait()
        idx_copy(2 * i + 3, 1).start()

    phase(1)

    @pl.when(jnp.logical_not(more))
    def _():
        for t in range(tb):
            wait_token(0, t)


def _gather(idx_tm, gate_tm, h2, x1, mod, mod_index, table):
    n, d = x1.shape
    nchunk = d // LANES
    tb = TB_GATHER
    step = 2 * tb
    tile_rows = 2 * nchunk
    tok_rows = PEER_SEL * tile_rows
    rep = (jnp.arange(tok_rows)[None, :] // tile_rows == jnp.arange(PEER_SEL)[:, None]).astype(MXU_DTYPE)
    row = lambda i: (i, 0)
    const = lambda i: (0, 0)
    return pl.pallas_call(
        functools.partial(_gather_kernel, tb=tb),
        grid=(n // step,),
        in_specs=[pl.BlockSpec(memory_space=pl.ANY),
                  pl.BlockSpec((step, PEER_SEL), row),
                  pl.BlockSpec((step, d), row), pl.BlockSpec((step, d), row),
                  pl.BlockSpec((1, N_MOD, d), mod_index),
                  pl.BlockSpec((PEER_SEL, tok_rows), const),
                  pl.BlockSpec((tok_rows, PEER_SEL), const),
                  pl.BlockSpec(memory_space=pl.ANY)],
        out_specs=pl.BlockSpec((step, d), row),
        out_shape=jax.ShapeDtypeStruct((n, d), F32),
        scratch_shapes=[pltpu.SMEM((2, tb, PEER_SEL), jnp.int32),
                        pltpu.VMEM((2, tb * tok_rows, LANES), MXU_DTYPE),
                        pltpu.VMEM((SUBLANES, tok_rows), F32),
                        pltpu.SemaphoreType.DMA((2,)),
                        pltpu.SemaphoreType.DMA((2 * tb,))],
        compiler_params=pltpu.CompilerParams(dimension_semantics=("arbitrary",), vmem_limit_bytes=VMEM_LIMIT),
        name="peer_gather",
    )(idx_tm, gate_tm, h2, x1, mod, rep, rep.T, table)


def _peer(h2, x1, mod, mod_index, wq_t, sk, table):
    idx, gate = _router(h2, wq_t, sk)
    return _gather(idx.T, gate.T, h2, x1, mod, mod_index, table)


def _final_norm_kernel(x_ref, g_ref, o_ref):
    o_ref[...] = _rms(x_ref[...]) * g_ref[...]


def _final_norm(x2, g):
    n, d = x2.shape
    tm = 2 * TM_DENSE
    return pl.pallas_call(
        _final_norm_kernel,
        grid=(n // tm,),
        in_specs=[pl.BlockSpec((tm, d), lambda i: (i, 0)), pl.BlockSpec((1, d), lambda i: (0, 0))],
        out_specs=pl.BlockSpec((tm, d), lambda i: (i, 0)),
        out_shape=jax.ShapeDtypeStruct((n, d), F32),
        name="final_rmsnorm",
    )(x2, g)


def _pack_table(down, up):
    e, d = down.shape
    chunks = (e, d // LANES, LANES)
    return jnp.concatenate([down.reshape(chunks), up.reshape(chunks)], axis=1).astype(MXU_DTYPE)


def _rope_tables(seq):
    quarter = HEAD_DIM // 4
    inv = jnp.power(ROPE_BASE, -jnp.arange(quarter, dtype=F32) / quarter)
    t = jnp.arange(seq)
    row = (t // GRID_W).astype(F32)
    col = (t % GRID_W).astype(F32)
    ang_r = row[:, None] * inv[None, :]
    ang_c = col[:, None] * inv[None, :]
    cos = jnp.concatenate([jnp.cos(ang_r)] * 2 + [jnp.cos(ang_c)] * 2, axis=1)
    sin = jnp.concatenate([-jnp.sin(ang_r), jnp.sin(ang_r), -jnp.sin(ang_c), jnp.sin(ang_c)], axis=1)
    reps = LANES // HEAD_DIM
    return jnp.tile(cos, (1, reps)), jnp.tile(sin, (1, reps))


def _inproj_weights(w_in):
    d = w_in.shape[0]
    slots = []
    for h in range(ATTN_HEADS):
        g = h // GQA_GROUP
        blk = w_in[:, h * HEAD_DIM:(h + 1) * HEAD_DIM]
        z = jnp.zeros((d, HEAD_DIM), w_in.dtype)
        slots.append(jnp.concatenate([blk, z] if g == 0 else [z, blk], axis=1))
    return jnp.concatenate(slots + [w_in[:, Q_W:]], axis=1).astype(MXU_DTYPE)


def _outproj_weights(w_out):
    d = w_out.shape[1]
    slots = []
    for h in range(ATTN_HEADS):
        g = h // GQA_GROUP
        blk = w_out[h * HEAD_DIM:(h + 1) * HEAD_DIM]
        z = jnp.zeros((HEAD_DIM, d), w_out.dtype)
        slots.append(jnp.concatenate([blk, z] if g == 0 else [z, blk], axis=0))
    return jnp.concatenate(slots, axis=0).astype(MXU_DTYPE), w_out[Q_W:].astype(MXU_DTYPE)


def kernel(x, c, ctx, c_ctx, mod_w, mod_b, norm1_g, norm2_g, attn_in_w, attn_out_w, attn_sink, gmlp_w_s, gmlp_b_s, pool_w, pool_scale, peer_w_q, peer_sub_keys, peer_down, peer_up, final_g):
    batch, seq, d = x.shape
    n = batch * seq
    n_ctx = ctx.shape[1]
    depth = mod_w.shape[0]
    assert depth == 2 and d == D_MODEL
    assert seq % TM_DENSE == 0 and n_ctx % TM_DENSE == 0 and LANES % (2 * TB_GATHER) == 0
    assert n % (2 * TM_DENSE) == 0

    mod_rows = -(-(batch + 1) // SUBLANES) * SUBLANES
    cs = jnp.zeros((mod_rows, d), F32).at[:batch].set(c).at[batch].set(c_ctx)
    mods = _modulation(cs, mod_w, mod_b)
    seq_tiles = seq // TM_DENSE
    main_mod = lambda i: (i // seq_tiles, 0, 0)
    ctx_mod = lambda i: (batch, 0, 0)
    gather_mod = lambda i: (i // (seq // (2 * TB_GATHER)), 0, 0)

    x2 = x.reshape(n, d)
    row1 = lambda v: v.reshape(1, -1)

    w_ext = _inproj_weights(attn_in_w[0])
    cos, sin = _rope_tables(seq)
    m64 = jnp.kron(jnp.eye(GMLP_GROUPS, dtype=F32), jnp.full((GMLP_DIM, GMLP_DIM), 1.0 / GMLP_DIM, F32)
                   ).astype(MXU_DTYPE)
    q, k, v, u, gn = _inproj(x2, mods[0], main_mod, row1(norm1_g[0]), w_ext, cos, sin, m64, seq_tiles)
    ones = jnp.ones((n_ctx, LANES), F32)
    _, kx, vx, _, _ = _inproj(ctx.reshape(batch * n_ctx, d), mods[0], ctx_mod, row1(norm1_g[0]), w_ext,
                              ones, jnp.zeros_like(ones), m64, n_ctx // TM_DENSE)
    ws = gmlp_w_s[0].astype(MXU_DTYPE)
    bt = gmlp_b_s[0]
    bs = jnp.stack([jnp.concatenate([jnp.broadcast_to(bt[2 * j][:, None], (CHUNK, GMLP_DIM)),
                                     jnp.broadcast_to(bt[2 * j + 1][:, None], (CHUNK, GMLP_DIM))], axis=1)
                    for j in range(GMLP_GROUPS // 2)])
    attn, mix = _attn_gmlp(attn_sink[0], q, k, v, kx, vx, u, gn, ws, bs, batch, seq)
    wa, wm = _outproj_weights(attn_out_w[0])
    x1, h2 = _outproj(x2, attn, mix, wa, wm, mods[0], main_mod, row1(norm2_g[0]))
    xl = _peer(h2, x1, mods[0], gather_mod, peer_w_q[0].T.astype(MXU_DTYPE),
               peer_sub_keys[0].astype(MXU_DTYPE), _pack_table(peer_down[0], peer_up[0]))

    x1, h2 = _pool(xl, mods[1], main_mod, row1(norm1_g[1]), row1(norm2_g[1]), pool_w[0].astype(MXU_DTYPE),
                   row1(pool_scale[0]), seq)
    xl = _peer(h2, x1, mods[1], gather_mod, peer_w_q[1].T.astype(MXU_DTYPE),
               peer_sub_keys[1].astype(MXU_DTYPE), _pack_table(peer_down[1], peer_up[1]))
    return _final_norm(xl, row1(final_g)).reshape(batch, seq, d)
```
